```python
import jax, jax.numpy as jnp
from jax import lax
import numpy as np

D_MODEL = 1024
BATCH = 8
SEQ = 8192
DEPTH = 2

MEM_LEN = 256
N_XA_HEADS = 4
XA_HEAD_DIM = D_MODEL // N_XA_HEADS
POOL_WINDOWS = (2, 4, 8, 16)
N_POOL_GROUPS = len(POOL_WINDOWS)
POOL_GROUP_DIM = D_MODEL // N_POOL_GROUPS
CONV_WIDTH = 31
D_FF = (D_MODEL * 7) // 2
N_EXPERTS = 8
TOP_K = 2
N_MIXERS = 2
N_EVEN = (DEPTH + 1) // 2
N_ODD = DEPTH // 2
EPS = 1e-6

kernel_name = "hybrid_pool_conformer_memxattn_moe"


def rms_norm(x, g):
    xf = x.astype(jnp.float32)
    y = xf * lax.rsqrt(jnp.mean(xf * xf, axis=-1, keepdims=True) + EPS)
    return (y * g.astype(jnp.float32)).astype(x.dtype)


def layer_norm(x, g, b):
    xf = x.astype(jnp.float32)
    mu = jnp.mean(xf, axis=-1, keepdims=True)
    xc = xf - mu
    var = jnp.mean(xc * xc, axis=-1, keepdims=True)
    y = xc * lax.rsqrt(var + EPS)
    return (y * g.astype(jnp.float32) + b.astype(jnp.float32)).astype(x.dtype)


def pool_mixer(h, w_grp, b_grp, scale):
    B, S, D = h.shape
    hf = h.astype(jnp.float32)
    csum = jnp.cumsum(hf, axis=1).reshape(B, S, N_POOL_GROUPS, POOL_GROUP_DIM)
    pos = jnp.arange(S)
    pooled = []
    for g, w in enumerate(POOL_WINDOWS):
        c = csum[:, :, g]
        c_prev = jnp.pad(c, ((0, 0), (w, 0), (0, 0)))[:, :S]
        cnt = jnp.minimum(pos + 1, w).astype(jnp.float32)[None, :, None]
        pooled.append((c - c_prev) / cnt)
    pooled = jnp.stack(pooled, axis=2) - hf.reshape(B, S, N_POOL_GROUPS, POOL_GROUP_DIM)
    y = jnp.einsum("bsgc,gcd->bsgd", pooled.astype(h.dtype), w_grp) + b_grp
    return y.reshape(B, S, D) * scale


def conv_module(h, pw1_w, pw1_b, dw_w, dw_b, ln_g, ln_b, pw2_w, pw2_b):
    D = h.shape[-1]
    a = h @ pw1_w + pw1_b
    u = a[..., :D] * jax.nn.sigmoid(a[..., D:])
    u_pad = jnp.pad(u, ((0, 0), (CONV_WIDTH - 1, 0), (0, 0)))
    v = lax.conv_general_dilated(
        u_pad, dw_w[:, None, :], window_strides=(1,), padding="VALID",
        dimension_numbers=("NWC", "WIO", "NWC"), feature_group_count=D) + dw_b
    v = jax.nn.silu(layer_norm(v, ln_g, ln_b))
    return v @ pw2_w + pw2_b


def mem_cross_attention(h, mem_n, wq, wkv, wo):
    B, S, D = h.shape
    M = mem_n.shape[1]
    q = (h @ wq).reshape(B, S, N_XA_HEADS, XA_HEAD_DIM)
    kv = mem_n @ wkv
    k = kv[..., :D].reshape(B, M, N_XA_HEADS, XA_HEAD_DIM)
    v = kv[..., D:].reshape(B, M, N_XA_HEADS, XA_HEAD_DIM)
    scores = jnp.einsum("bshd,bmhd->bhsm", q, k).astype(jnp.float32) * (XA_HEAD_DIM ** -0.5)
    p = jax.nn.softmax(scores, axis=-1).astype(v.dtype)
    o = jnp.einsum("bhsm,bmhd->bshd", p, v).reshape(B, S, D)
    return o @ wo


def swiglu(h, w_gu, w_down):
    gu = h @ w_gu
    return (jax.nn.silu(gu[..., :D_FF]) * gu[..., D_FF:]) @ w_down


def moe_swiglu(h, w_router, w_gu, w_down):
    logits = (h @ w_router).astype(jnp.float32)
    top_vals, top_idx = lax.top_k(logits, TOP_K)
    top_w = jax.nn.softmax(top_vals, axis=-1)
    gates = jnp.sum(jax.nn.one_hot(top_idx, N_EXPERTS, dtype=jnp.float32) * top_w[..., None], axis=-2)
    out = jnp.zeros_like(h)
    for e in range(N_EXPERTS):
        out = out + swiglu(h, w_gu[e], w_down[e]) * gates[..., e:e + 1].astype(h.dtype)
    return out


def setup_inputs(seed: int = 0) -> dict:
    key = jax.random.key(seed)
    ks = iter(jax.random.split(key, 32))
    D, F, E = D_MODEL, D_FF, N_EXPERTS

    def nrm(shape, scale):
        return jax.random.normal(next(ks), shape, jnp.float32) * scale

    def gain(shape):
        return 1.0 + nrm(shape, 0.05)

    return {
        "x": nrm((BATCH, SEQ, D), 1.0),
        "mem": nrm((BATCH, MEM_LEN, D), 1.0),
        "mem_norm_g": gain((D,)),
        "mix_pre_g": gain((DEPTH, D)),
        "mix_post_g": gain((DEPTH, D)),
        "xa_pre_g": gain((DEPTH, D)),
        "xa_post_g": gain((DEPTH, D)),
        "ffn_pre_g": gain((DEPTH, D)),
        "ffn_post_g": gain((DEPTH, D)),
        "xa_wq": nrm((DEPTH, D, D), D ** -0.5),
        "xa_wkv": nrm((DEPTH, D, 2 * D), D ** -0.5),
        "xa_wo": nrm((DEPTH, D, D), D ** -0.5),
        "pool_w": nrm((N_EVEN, N_POOL_GROUPS, POOL_GROUP_DIM, POOL_GROUP_DIM), POOL_GROUP_DIM ** -0.5),
        "pool_b": nrm((N_EVEN, N_POOL_GROUPS, POOL_GROUP_DIM), 0.02),
        "pool_scale": 1.0 + nrm((N_EVEN, D), 0.1),
        "conv_pw1_w": nrm((N_ODD, D, 2 * D), D ** -0.5),
        "conv_pw1_b": nrm((N_ODD, 2 * D), 0.02),
        "conv_dw_w": nrm((N_ODD, CONV_WIDTH, D), CONV_WIDTH ** -0.5),
        "conv_dw_b": nrm((N_ODD, D), 0.02),
        "conv_ln_g": gain((N_ODD, D)),
        "conv_ln_b": nrm((N_ODD, D), 0.02),
        "conv_pw2_w": nrm((N_ODD, D, D), D ** -0.5),
        "conv_pw2_b": nrm((N_ODD, D), 0.02),
        "ffn_w_gu": nrm((N_EVEN, D, 2 * F), D ** -0.5),
        "ffn_w_down": nrm((N_EVEN, F, D), F ** -0.5),
        "moe_router": nrm((N_ODD, D, E), D ** -0.5),
        "moe_w_gu": nrm((N_ODD, E, D, 2 * F), D ** -0.5),
        "moe_w_down": nrm((N_ODD, E, F, D), F ** -0.5),
    }


def reference(x, mem, mem_norm_g, mix_pre_g, mix_post_g, xa_pre_g, xa_post_g,
              ffn_pre_g, ffn_post_g, xa_wq, xa_wkv, xa_wo, pool_w, pool_b,
              pool_scale, conv_pw1_w, conv_pw1_b, conv_dw_w, conv_dw_b,
              conv_ln_g, conv_ln_b, conv_pw2_w, conv_pw2_b, ffn_w_gu,
              ffn_w_down, moe_router, moe_w_gu, moe_w_down):
    mem_n = rms_norm(mem, mem_norm_g)
    for i in range(DEPTH):
        j = i // 2
        hn = rms_norm(x, mix_pre_g[i])
        if i % N_MIXERS == 0:
            y = pool_mixer(hn, pool_w[j], pool_b[j], pool_scale[j])
        else:
            y = conv_module(hn, conv_pw1_w[j], conv_pw1_b[j], conv_dw_w[j], conv_dw_b[j],
                            conv_ln_g[j], conv_ln_b[j], conv_pw2_w[j], conv_pw2_b[j])
        x = x + rms_norm(y, mix_post_g[i])
        y = mem_cross_attention(rms_norm(x, xa_pre_g[i]), mem_n, xa_wq[i], xa_wkv[i], xa_wo[i])
        x = x + rms_norm(y, xa_post_g[i])
        hn = rms_norm(x, ffn_pre_g[i])
        if i % 2 == 0:
            y = swiglu(hn, ffn_w_gu[j], ffn_w_down[j])
        else:
            y = moe_swiglu(hn, moe_router[j], moe_w_gu[j], moe_w_down[j])
        x = x + rms_norm(y, ffn_post_g[i])
    return x
```

```python
import functools

import jax
import jax.numpy as jnp
from jax import lax
from jax.experimental import pallas as pl
from jax.experimental.pallas import tpu as pltpu

EPS = 1e-6
N_XA_HEADS = 4
POOL_WINDOWS = (2, 4, 8, 16)
TOP_K = 2

LANES = 128
SUBLANES = 8
POOL_HALO = 16
CONV_HALO = 32
CONV_ROWS = 64
ZERO_ROWS = 256
VMEM_LIMIT = 56 * 1024 * 1024

F32 = jnp.float32
BF16 = jnp.bfloat16


def _rms(x, g):
    return x * lax.rsqrt(jnp.mean(x * x, axis=-1, keepdims=True) + EPS) * g


def _params(*sem):
    return pltpu.CompilerParams(dimension_semantics=sem, vmem_limit_bytes=VMEM_LIMIT)


def _row(v):
    return v.reshape(1, -1).astype(F32)


def _full(shape):
    nd = len(shape)
    return pl.BlockSpec(shape, lambda *_: (0,) * nd)


def _kv_kernel(mem_ref, g_ref, wkv_ref, kt_ref, v_ref):
    d = mem_ref.shape[-1]
    mn = _rms(mem_ref[0], g_ref[...]).astype(BF16)
    kv = jnp.dot(mn, wkv_ref[0], preferred_element_type=F32)
    kt_ref[0, 0] = kv[:, :d].T.astype(BF16)
    v_ref[0, 0] = kv[:, d:].astype(BF16)


def _memory_kv(mem, mem_norm_g, wkv):
    depth, d, _ = wkv.shape
    b, m, _ = mem.shape
    return pl.pallas_call(
        _kv_kernel,
        grid=(depth, b),
        in_specs=[
            pl.BlockSpec((1, m, d), lambda l, i: (i, 0, 0)),
            _full((1, d)),
            pl.BlockSpec((1, d, 2 * d), lambda l, i: (l, 0, 0)),
        ],
        out_specs=[
            pl.BlockSpec((1, 1, d, m), lambda l, i: (l, i, 0, 0)),
            pl.BlockSpec((1, 1, m, d), lambda l, i: (l, i, 0, 0)),
        ],
        out_shape=[
            jax.ShapeDtypeStruct((depth, b, d, m), BF16),
            jax.ShapeDtypeStruct((depth, b, m, d), BF16),
        ],
        compiler_params=_params("arbitrary", "arbitrary"),
        name="memory_kv",
    )(mem, _row(mem_norm_g), wkv.astype(BF16))


def _pool_kernel(x_ref, gpre_ref, w_ref, b_ref, scale_ref, gpost_ref, o_ref, ext_ref, *, tiles_per_seq):
    tm, d = x_ref.shape
    c = d // len(POOL_WINDOWS)
    t = pl.program_id(0) % tiles_per_seq

    @pl.when(t == 0)
    def _():
        ext_ref[0:POOL_HALO, :] = jnp.zeros((POOL_HALO, d), F32)

    x = x_ref[...]
    hn = _rms(x, gpre_ref[...])
    ext_ref[POOL_HALO:, :] = hn

    pos = (t * tm + lax.broadcasted_iota(jnp.int32, (tm, 1), 0) + 1).astype(F32)
    ys = []
    for g, w in enumerate(POOL_WINDOWS):
        s = ext_ref[:, g * c:(g + 1) * c]
        span = 1
        while span < w:
            s = s + pltpu.roll(s, span, 0)
            span *= 2
        pooled = s[POOL_HALO:, :] / jnp.minimum(pos, float(w)) - hn[:, g * c:(g + 1) * c]
        y = jnp.dot(pooled.astype(BF16), w_ref[g], preferred_element_type=F32)
        ys.append(y + b_ref[:, g * c:(g + 1) * c])
    y = jnp.concatenate(ys, axis=-1) * scale_ref[...]
    ext_ref[0:POOL_HALO, :] = hn[tm - POOL_HALO:, :]
    o_ref[...] = x + _rms(y, gpost_ref[...])


def _pool_layer(x, gpre, w_grp, b_grp, scale, gpost, *, seq, tm):
    n, d = x.shape
    g, c, _ = w_grp.shape
    tok = pl.BlockSpec((tm, d), lambda i: (i, 0))
    return pl.pallas_call(
        functools.partial(_pool_kernel, tiles_per_seq=seq // tm),
        grid=(n // tm,),
        in_specs=[tok, _full((1, d)), _full((g, c, c)), _full((1, d)), _full((1, d)), _full((1, d))],
        out_specs=tok,
        out_shape=jax.ShapeDtypeStruct((n, d), F32),
        scratch_shapes=[pltpu.VMEM((tm + POOL_HALO, d), F32)],
        compiler_params=_params("arbitrary"),
        name="pool_mixer",
    )(x, _row(gpre), w_grp.astype(BF16), _row(b_grp), _row(scale), _row(gpost))


def _xattn_kernel(x_ref, kt_ref, v_ref, wq_ref, wo_ref, gpre_ref, gpost_ref, o_ref):
    d = x_ref.shape[-1]
    hd = d // N_XA_HEADS
    x = x_ref[...]
    h = _rms(x, gpre_ref[...]).astype(BF16)
    q = jnp.dot(h, wq_ref[...], preferred_element_type=F32) * (hd ** -0.5)
    heads = []
    for i in range(N_XA_HEADS):
        sl = slice(i * hd, (i + 1) * hd)
        s = jnp.dot(q[:, sl].astype(BF16), kt_ref[0, sl, :], preferred_element_type=F32)
        p = jnp.exp(s - jnp.max(s, axis=-1, keepdims=True))
        p = p / jnp.sum(p, axis=-1, keepdims=True)
        heads.append(jnp.dot(p.astype(BF16), v_ref[0, :, sl], preferred_element_type=F32).astype(BF16))
    y = jnp.dot(jnp.concatenate(heads, axis=-1), wo_ref[...], preferred_element_type=F32)
    o_ref[...] = x + _rms(y, gpost_ref[...])


def _xattn_layer(x, kt, v, wq, wo, gpre, gpost, *, seq, tm):
    n, d = x.shape
    m = kt.shape[-1]
    tps = seq // tm
    tok = pl.BlockSpec((tm, d), lambda i: (i, 0))
    return pl.pallas_call(
        _xattn_kernel,
        grid=(n // tm,),
        in_specs=[
            tok,
            pl.BlockSpec((1, d, m), lambda i: (i // tps, 0, 0)),
            pl.BlockSpec((1, m, d), lambda i: (i // tps, 0, 0)),
            _full((d, d)), _full((d, d)), _full((1, d)), _full((1, d)),
        ],
        out_specs=tok,
        out_shape=jax.ShapeDtypeStruct((n, d), F32),
        compiler_params=_params("arbitrary"),
        name="mem_xattn",
    )(x, kt, v, wq.astype(BF16), wo.astype(BF16), _row(gpre), _row(gpost))


def _swiglu_step(x_ref, gpre_ref, wg_ref, wu_ref, wd_ref, hn_ref, acc_ref):
    j = pl.program_id(1)

    @pl.when(j == 0)
    def _():
        hn_ref[...] = _rms(x_ref[...], gpre_ref[...]).astype(BF16)
        acc_ref[...] = jnp.zeros_like(acc_ref)

    hn = hn_ref[...]
    gate = jnp.dot(hn, wg_ref[...], preferred_element_type=F32)
    up = jnp.dot(hn, wu_ref[...], preferred_element_type=F32)
    act = (gate * jax.nn.sigmoid(gate) * up).astype(BF16)
    acc_ref[...] += jnp.dot(act, wd_ref[...], preferred_element_type=F32)


def _ffn_kernel(x_ref, gpre_ref, wg_ref, wu_ref, wd_ref, gpost_ref, o_ref, hn_ref, acc_ref):
    _swiglu_step(x_ref, gpre_ref, wg_ref, wu_ref, wd_ref, hn_ref, acc_ref)

    @pl.when(pl.program_id(1) == pl.num_programs(1) - 1)
    def _():
        o_ref[...] = x_ref[...] + _rms(acc_ref[...], gpost_ref[...])


def _ffn_layer(x, gpre, w_gu, w_down, gpost, *, tm, fb):
    n, d = x.shape
    f = w_down.shape[0]
    nf = f // fb
    tok = pl.BlockSpec((tm, d), lambda i, j: (i, 0))
    w_gu = w_gu.astype(BF16)
    return pl.pallas_call(
        _ffn_kernel,
        grid=(n // tm, nf),
        in_specs=[
            tok, _full((1, d)),
            pl.BlockSpec((d, fb), lambda i, j: (0, j)),
            pl.BlockSpec((d, fb), lambda i, j: (0, nf + j)),
            pl.BlockSpec((fb, d), lambda i, j: (j, 0)),
            _full((1, d)),
        ],
        out_specs=tok,
        out_shape=jax.ShapeDtypeStruct((n, d), F32),
        scratch_shapes=[pltpu.VMEM((tm, d), BF16), pltpu.VMEM((tm, d), F32)],
        compiler_params=_params("arbitrary", "arbitrary"),
        name="dense_swiglu",
    )(x, _row(gpre), w_gu, w_gu, w_down.astype(BF16), _row(gpost))


def _experts_kernel(tile_expert_ref, ntiles_ref, x_ref, gpre_ref, wg_ref, wu_ref, wd_ref, y_ref, hn_ref, acc_ref):
    i = pl.program_id(0)
    last = pl.program_id(1) == pl.num_programs(1) - 1
    live = i < ntiles_ref[0]

    @pl.when(live)
    def _():
        _swiglu_step(x_ref, gpre_ref, wg_ref.at[0], wu_ref.at[0], wd_ref.at[0], hn_ref, acc_ref)

    @pl.when(live & last)
    def _():
        y_ref[...] = acc_ref[...]

    @pl.when(jnp.logical_not(live) & last)
    def _():
        y_ref[...] = jnp.zeros_like(y_ref)


def _experts_layer(xs, gpre, w_gu, w_down, tile_expert, ntiles, *, te, fb):
    r, d = xs.shape
    _, f, _ = w_down.shape
    nf = f // fb
    w_gu = w_gu.astype(BF16)

    def x_map(i, j, te_ref, nt_ref):
        return (jnp.minimum(i, nt_ref[0] - 1), 0)

    grid_spec = pltpu.PrefetchScalarGridSpec(
        num_scalar_prefetch=2,
        grid=(r // te, nf),
        in_specs=[
            pl.BlockSpec((te, d), x_map),
            pl.BlockSpec((1, d), lambda i, j, te_ref, nt_ref: (0, 0)),
            pl.BlockSpec((1, d, fb), lambda i, j, te_ref, nt_ref: (te_ref[i], 0, j)),
            pl.BlockSpec((1, d, fb), lambda i, j, te_ref, nt_ref: (te_ref[i], 0, nf + j)),
            pl.BlockSpec((1, fb, d), lambda i, j, te_ref, nt_ref: (te_ref[i], j, 0)),
        ],
        out_specs=pl.BlockSpec((te, d), lambda i, j, te_ref, nt_ref: (i, 0)),
        scratch_shapes=[pltpu.VMEM((te, d), BF16), pltpu.VMEM((te, d), F32)],
    )
    return pl.pallas_call(
        _experts_kernel,
        grid_spec=grid_spec,
        out_shape=jax.ShapeDtypeStruct((r, d), F32),
        compiler_params=_params("arbitrary", "arbitrary"),
        name="moe_experts",
    )(tile_expert, ntiles, xs, _row(gpre), w_gu, w_gu, w_down.astype(BF16))


def _conv_kernel(x_ref, gpre_ref, w1_ref, b1_ref, dw_ref, dwb_ref, lng_ref, lnb_ref, w2_ref, b2_ref, gpost_ref,
                 o_ref, uext_ref, v_ref, *, tiles_per_seq, width):
    tm, d = x_ref.shape
    nblk = d // LANES
    t = pl.program_id(0) % tiles_per_seq

    @pl.when(t == 0)
    def _():
        uext_ref[:, 0:CONV_HALO, :] = jnp.zeros((nblk, CONV_HALO, LANES), F32)

    x = x_ref[...]
    h = _rms(x, gpre_ref[...]).astype(BF16)
    a = jnp.dot(h, w1_ref[...], preferred_element_type=F32) + b1_ref[...]
    u = a[:, :d] * jax.nn.sigmoid(a[:, d:])
    for c in range(nblk):
        uext_ref[c, CONV_HALO:, :] = u[:, c * LANES:(c + 1) * LANES]

    first = CONV_HALO - (width - 1)

    def lane_block(c, carry):
        taps = [dw_ref[c, k:k + 1, :] for k in range(width)]
        for r0 in range(0, tm, CONV_ROWS):
            acc = taps[0] * uext_ref[c, r0 + first:r0 + first + CONV_ROWS, :]
            for k in range(1, width):
                acc = acc + taps[k] * uext_ref[c, r0 + first + k:r0 + first + k + CONV_ROWS, :]
            v_ref[c, r0:r0 + CONV_ROWS, :] = acc
        uext_ref[c, 0:CONV_HALO, :] = uext_ref[c, tm:tm + CONV_HALO, :]
        return carry

    lax.fori_loop(0, nblk, lane_block, 0)

    v = jnp.concatenate([v_ref[c] for c in range(nblk)], axis=-1) + dwb_ref[...]
    mu = jnp.mean(v, axis=-1, keepdims=True)
    vc = v - mu
    var = jnp.mean(vc * vc, axis=-1, keepdims=True)
    z = vc * lax.rsqrt(var + EPS) * lng_ref[...] + lnb_ref[...]
    z = (z * jax.nn.sigmoid(z)).astype(BF16)
    y = jnp.dot(z, w2_ref[...], preferred_element_type=F32) + b2_ref[...]
    o_ref[...] = x + _rms(y, gpost_ref[...])


def _conv_layer(x, gpre, pw1_w, pw1_b, dw_w, dw_b, ln_g, ln_b, pw2_w, pw2_b, gpost, *, seq, tm):
    n, d = x.shape
    width = dw_w.shape[0]
    nblk = d // LANES
    kpad = -(-width // SUBLANES) * SUBLANES
    dw = jnp.pad(dw_w, ((0, kpad - width), (0, 0))).reshape(kpad, nblk, LANES).transpose(1, 0, 2)
    tok = pl.BlockSpec((tm, d), lambda i: (i, 0))
    vec = _full((1, d))
    return pl.pallas_call(
        functools.partial(_conv_kernel, tiles_per_seq=seq // tm, width=width),
        grid=(n // tm,),
        in_specs=[tok, vec, _full((d, 2 * d)), _full((1, 2 * d)), _full((nblk, kpad, LANES)),
                  vec, vec, vec, _full((d, d)), vec, vec],
        out_specs=tok,
        out_shape=jax.ShapeDtypeStruct((n, d), F32),
        scratch_shapes=[pltpu.VMEM((nblk, tm + CONV_HALO, LANES), F32), pltpu.VMEM((nblk, tm, LANES), F32)],
        compiler_params=_params("arbitrary"),
        name="conv_module",
    )(x, _row(gpre), pw1_w.astype(BF16), _row(pw1_b), dw, _row(dw_b), _row(ln_g), _row(ln_b),
      pw2_w.astype(BF16), _row(pw2_b), _row(gpost))


def _route_kernel(x_ref, gpre_ref, wr_ref, meta_ref, cnt_ref, carry_ref, *, n_experts):
    tm = x_ref.shape[0]

    @pl.when(pl.program_id(0) == 0)
    def _():
        carry_ref[...] = jnp.zeros_like(carry_ref)

    hn = _rms(x_ref[...], gpre_ref[...]).astype(BF16)
    logits = jnp.dot(hn, wr_ref[...], preferred_element_type=F32)
    lane = lax.broadcasted_iota(jnp.int32, (tm, LANES), 1).astype(F32)
    logits = jnp.where(lane < n_experts, logits, -jnp.inf)
    m1 = jnp.max(logits, axis=-1, keepdims=True)
    e1 = jnp.min(jnp.where(logits == m1, lane, float(LANES)), axis=-1, keepdims=True)
    rest = jnp.where(lane == e1, -jnp.inf, logits)
    m2 = jnp.max(rest, axis=-1, keepdims=True)
    e2 = jnp.min(jnp.where(rest == m2, lane, float(LANES)), axis=-1, keepdims=True)
    ex = jnp.exp(m2 - m1)
    w1 = 1.0 / (1.0 + ex)
    w2 = ex / (1.0 + ex)

    hit1 = lane == e1
    hit2 = lane == e2
    onehot = jnp.where(hit1 | hit2, 1.0, 0.0)
    rows = lax.broadcasted_iota(jnp.int32, (tm, tm), 0)
    cols = lax.broadcasted_iota(jnp.int32, (tm, tm), 1)
    below = jnp.where(rows > cols, 1.0, 0.0).astype(BF16)
    before = jnp.dot(below, onehot.astype(BF16), preferred_element_type=F32) + carry_ref[0:1, :]
    r1 = jnp.sum(jnp.where(hit1, before, 0.0), axis=-1, keepdims=True)
    r2 = jnp.sum(jnp.where(hit2, before, 0.0), axis=-1, keepdims=True)
    carry_ref[...] = carry_ref[...] + jnp.sum(onehot, axis=0, keepdims=True)
    cnt_ref[...] = carry_ref[...]

    meta = jnp.zeros((tm, LANES), F32)
    for k, val in enumerate((e1, e2, r1, r2, w1, w2)):
        meta = jnp.where(lane == float(k), val, meta)
    meta_ref[...] = meta[:, :meta_ref.shape[1]]


def _route(x, gpre, w_router, *, tm):
    n, d = x.shape
    e = w_router.shape[1]
    wr = jnp.pad(w_router, ((0, 0), (0, LANES - e))).astype(BF16)
    return pl.pallas_call(
        functools.partial(_route_kernel, n_experts=e),
        grid=(n // tm,),
        in_specs=[pl.BlockSpec((tm, d), lambda i: (i, 0)), _full((1, d)), _full((d, LANES))],
        out_specs=[pl.BlockSpec((tm, SUBLANES), lambda i: (i, 0)), _full((SUBLANES, LANES))],
        out_shape=[jax.ShapeDtypeStruct((n, SUBLANES), F32), jax.ShapeDtypeStruct((SUBLANES, LANES), F32)],
        scratch_shapes=[pltpu.VMEM((SUBLANES, LANES), F32)],
        compiler_params=_params("arbitrary"),
        name="moe_route",
    )(x, _row(gpre), wr)


def _dispatch_kernel(dest_ref, pad_start_ref, pad_len_ref, x_ref, xs_ref, zero_ref, sem, zsem):
    tm, d = x_ref.shape
    i = pl.program_id(0)

    def row_copy(r, k):
        return pltpu.make_async_copy(x_ref.at[pl.ds(r, 1), :], xs_ref.at[pl.ds(dest_ref[0, 0, k * tm + r], 1), :], sem)

    def start(r, carry):
        for k in range(TOP_K):
            row_copy(r, k).start()
        return carry

    lax.fori_loop(0, tm, start, 0)

    @pl.when(i == pl.num_programs(0) - 1)
    def _():
        zero_ref[...] = jnp.zeros_like(zero_ref)
        zrows = zero_ref.shape[0]
        n_experts = pad_start_ref.shape[0] - 1
        for e in range(n_experts + 1):
            def zero_copy(r, e=e):
                if e < n_experts:
                    return pltpu.make_async_copy(zero_ref.at[pl.ds(0, 1), :],
                                                 xs_ref.at[pl.ds(pad_start_ref[e] + r, 1), :], zsem)
                row0 = pl.multiple_of(pad_start_ref[e] + r * zrows, zrows)
                return pltpu.make_async_copy(zero_ref, xs_ref.at[pl.ds(row0, zrows), :], zsem)

            lax.fori_loop(0, pad_len_ref[e], lambda r, c: (zero_copy(r).start(), c)[1], 0)
            lax.fori_loop(0, pad_len_ref[e], lambda r, c: (zero_copy(r).wait(), c)[1], 0)

    def wait(r, carry):
        for k in range(TOP_K):
            row_copy(r, k).wait()
        return carry

    lax.fori_loop(0, tm, wait, 0)


def _dispatch(x, dest, pad_start, pad_len, *, rows, tm, zrows):
    n, d = x.shape
    return pl.pallas_call(
        _dispatch_kernel,
        grid=(n // tm,),
        in_specs=[
            pl.BlockSpec((1, 1, TOP_K * tm), lambda i: (i, 0, 0), memory_space=pltpu.SMEM),
            pl.BlockSpec(memory_space=pltpu.SMEM),
            pl.BlockSpec(memory_space=pltpu.SMEM),
            pl.BlockSpec((tm, d), lambda i: (i, 0)),
        ],
        out_specs=pl.BlockSpec(memory_space=pl.ANY),
        out_shape=jax.ShapeDtypeStruct((rows, d), F32),
        scratch_shapes=[pltpu.VMEM((zrows, d), F32), pltpu.SemaphoreType.DMA, pltpu.SemaphoreType.DMA],
        compiler_params=_params("arbitrary"),
        name="moe_dispatch",
    )(dest, pad_start, pad_len, x)


def _combine_kernel(dest_ref, x_ref, meta_ref, ys_ref, gpost_ref, o_ref, buf_ref, sem):
    tm, d = x_ref.shape

    def row_copy(r, k):
        return pltpu.make_async_copy(ys_ref.at[pl.ds(dest_ref[0, 0, k * tm + r], 1), :], buf_ref.at[k, pl.ds(r, 1), :], sem)

    def start(r, carry):
        for k in range(TOP_K):
            row_copy(r, k).start()
        return carry

    def wait(r, carry):
        for k in range(TOP_K):
            row_copy(r, k).wait()
        return carry

    lax.fori_loop(0, tm, start, 0)
    lax.fori_loop(0, tm, wait, 0)
    meta = meta_ref[...]
    y = buf_ref[0] * meta[:, 4:5] + buf_ref[1] * meta[:, 5:6]
    o_ref[...] = x_ref[...] + _rms(y, gpost_ref[...])


def _combine(x, dest, meta, ys, gpost, *, tm):
    n, d = x.shape
    tok = pl.BlockSpec((tm, d), lambda i: (i, 0))
    return pl.pallas_call(
        _combine_kernel,
        grid=(n // tm,),
        in_specs=[
            pl.BlockSpec((1, 1, TOP_K * tm), lambda i: (i, 0, 0), memory_space=pltpu.SMEM),
            tok,
            pl.BlockSpec((tm, SUBLANES), lambda i: (i, 0)),
            pl.BlockSpec(memory_space=pl.ANY),
            _full((1, d)),
        ],
        out_specs=tok,
        out_shape=jax.ShapeDtypeStruct((n, d), F32),
        scratch_shapes=[pltpu.VMEM((TOP_K, tm, d), F32), pltpu.SemaphoreType.DMA],
        compiler_params=_params("arbitrary"),
        name="moe_combine",
    )(dest, x, meta, ys, _row(gpost))


def _moe_layer(x, gpre, w_router, w_gu, w_down, gpost, *, tm, te, fb):
    n, d = x.shape
    e = w_router.shape[1]
    meta, cnt = _route(x, gpre, w_router, tm=tm)

    counts = cnt[0, :e].astype(jnp.int32)
    padded = (counts + te - 1) // te * te
    ends = jnp.cumsum(padded)
    starts = ends - padded
    ids = meta[:, 0:TOP_K].astype(jnp.int32)
    ranks = meta[:, TOP_K:2 * TOP_K].astype(jnp.int32)
    dest = starts[ids] + ranks
    dest = dest.reshape(n // tm, tm, TOP_K).transpose(0, 2, 1).reshape(n // tm, 1, TOP_K * tm)
    rows = TOP_K * n + e * te
    ntiles = (ends[-1] // te).reshape(1)
    tile_expert = jnp.minimum(jnp.searchsorted(ends // te, jnp.arange(rows // te, dtype=jnp.int32), side="right"),
                              e - 1).astype(jnp.int32)

    zrows = min(te, ZERO_ROWS)
    assert te % zrows == 0 and rows % te == 0
    fill_start = jnp.concatenate([starts + counts, ends[-1:]])
    fill_len = jnp.concatenate([padded - counts, (rows - ends[-1:]) // zrows])
    xs = _dispatch(x, dest, fill_start, fill_len, rows=rows, tm=tm, zrows=zrows)
    ys = _experts_layer(xs, gpre, w_gu, w_down, tile_expert, ntiles, te=te, fb=fb)
    return _combine(x, dest, meta, ys, gpost, tm=tm)


def _tile(n, want):
    t = min(n, want)
    while n % t:
        t //= 2
    return t


@jax.jit
def kernel(x, mem, mem_norm_g, mix_pre_g, mix_post_g, xa_pre_g, xa_post_g, ffn_pre_g, ffn_post_g, xa_wq, xa_wkv, xa_wo, pool_w, pool_b, pool_scale, conv_pw1_w, conv_pw1_b, conv_dw_w, conv_dw_b, conv_ln_g, conv_ln_b, conv_pw2_w, conv_pw2_b, ffn_w_gu, ffn_w_down, moe_router, moe_w_gu, moe_w_down):
    b, s, d = x.shape
    depth = mix_pre_g.shape[0]
    f = ffn_w_down.shape[1]
    tm = _tile(s, 512)
    tf = _tile(s, 1024)
    fb = _tile(f, 512)
    kt, v = _memory_kv(mem, mem_norm_g, xa_wkv)
    h = x.reshape(b * s, d)
    for i in range(depth):
        j = i // 2
        if i % 2 == 0:
            h = _pool_layer(h, mix_pre_g[i], pool_w[j], pool_b[j], pool_scale[j], mix_post_g[i], seq=s, tm=tm)
        else:
            h = _conv_layer(h, mix_pre_g[i], conv_pw1_w[j], conv_pw1_b[j], conv_dw_w[j], conv_dw_b[j], conv_ln_g[j],
                            conv_ln_b[j], conv_pw2_w[j], conv_pw2_b[j], mix_post_g[i], seq=s, tm=tm)
        h = _xattn_layer(h, kt[i], v[i], xa_wq[i], xa_wo[i], xa_pre_g[i], xa_post_g[i], seq=s, tm=tm)
        if i % 2 == 0:
            h = _ffn_layer(h, ffn_pre_g[i], ffn_w_gu[j], ffn_w_down[j], ffn_post_g[i], tm=tf, fb=fb)
        else:
            h = _moe_layer(h, ffn_pre_g[i], moe_router[j], moe_w_gu[j], moe_w_down[j], ffn_post_g[i], tm=tm, te=tf, fb=fb)
    return h.reshape(b, s, d)
```

```python
import functools

import jax
import jax.numpy as jnp
from jax import lax
from jax.experimental import pallas as pl
from jax.experimental.pallas import tpu as pltpu

EPS = 1e-6
N_XA_HEADS = 4
POOL_WINDOWS = (2, 4, 8, 16)
TOP_K = 2

LANES = 128
SUBLANES = 8
POOL_HALO = 16
CONV_HALO = 32
CONV_ROWS = 64
ZERO_ROWS = 256
DMA_UNROLL = 8
DMA_PRIORITIES = 2
VMEM_LIMIT = 56 * 1024 * 1024

F32 = jnp.float32
BF16 = jnp.bfloat16


def _rms(x, g):
    return x * lax.rsqrt(jnp.mean(x * x, axis=-1, keepdims=True) + EPS) * g


def _params(*sem):
    return pltpu.CompilerParams(dimension_semantics=sem, vmem_limit_bytes=VMEM_LIMIT)


def _row(v):
    return v.reshape(1, -1).astype(F32)


def _full(shape):
    nd = len(shape)
    return pl.BlockSpec(shape, lambda *_: (0,) * nd)


def _kv_kernel(mem_ref, g_ref, wkv_ref, kt_ref, v_ref):
    d = mem_ref.shape[-1]
    mn = _rms(mem_ref[0], g_ref[...]).astype(BF16)
    kv = jnp.dot(mn, wkv_ref[0], preferred_element_type=F32)
    kt_ref[0, 0] = kv[:, :d].T.astype(BF16)
    v_ref[0, 0] = kv[:, d:].astype(BF16)


def _memory_kv(mem, mem_norm_g, wkv):
    depth, d, _ = wkv.shape
    b, m, _ = mem.shape
    return pl.pallas_call(
        _kv_kernel,
        grid=(depth, b),
        in_specs=[
            pl.BlockSpec((1, m, d), lambda l, i: (i, 0, 0)),
            _full((1, d)),
            pl.BlockSpec((1, d, 2 * d), lambda l, i: (l, 0, 0)),
        ],
        out_specs=[
            pl.BlockSpec((1, 1, d, m), lambda l, i: (l, i, 0, 0)),
            pl.BlockSpec((1, 1, m, d), lambda l, i: (l, i, 0, 0)),
        ],
        out_shape=[
            jax.ShapeDtypeStruct((depth, b, d, m), BF16),
            jax.ShapeDtypeStruct((depth, b, m, d), BF16),
        ],
        compiler_params=_params("arbitrary", "arbitrary"),
        name="memory_kv",
    )(mem, _row(mem_norm_g), wkv.astype(BF16))


def _pool_kernel(x_ref, gpre_ref, w_ref, b_ref, scale_ref, gpost_ref, o_ref, ext_ref, *, tiles_per_seq):
    tm, d = x_ref.shape
    c = d // len(POOL_WINDOWS)
    t = pl.program_id(0) % tiles_per_seq

    @pl.when(t == 0)
    def _():
        ext_ref[0:POOL_HALO, :] = jnp.zeros((POOL_HALO, d), F32)

    x = x_ref[...]
    hn = _rms(x, gpre_ref[...])
    ext_ref[POOL_HALO:, :] = hn

    pos = (t * tm + lax.broadcasted_iota(jnp.int32, (tm, 1), 0) + 1).astype(F32)
    ys = []
    for g, w in enumerate(POOL_WINDOWS):
        s = ext_ref[:, g * c:(g + 1) * c]
        span = 1
        while span < w:
            s = s + pltpu.roll(s, span, 0)
            span *= 2
        pooled = s[POOL_HALO:, :] / jnp.minimum(pos, float(w)) - hn[:, g * c:(g + 1) * c]
        y = jnp.dot(pooled.astype(BF16), w_ref[g], preferred_element_type=F32)
        ys.append(y + b_ref[:, g * c:(g + 1) * c])
    y = jnp.concatenate(ys, axis=-1) * scale_ref[...]
    ext_ref[0:POOL_HALO, :] = hn[tm - POOL_HALO:, :]
    o_ref[...] = x + _rms(y, gpost_ref[...])


def _pool_layer(x, gpre, w_grp, b_grp, scale, gpost, *, seq, tm):
    n, d = x.shape
    g, c, _ = w_grp.shape
    tok = pl.BlockSpec((tm, d), lambda i: (i, 0))
    return pl.pallas_call(
        functools.partial(_pool_kernel, tiles_per_seq=seq // tm),
        grid=(n // tm,),
        in_specs=[tok, _full((1, d)), _full((g, c, c)), _full((1, d)), _full((1, d)), _full((1, d))],
        out_specs=tok,
        out_shape=jax.ShapeDtypeStruct((n, d), F32),
        scratch_shapes=[pltpu.VMEM((tm + POOL_HALO, d), F32)],
        compiler_params=_params("arbitrary"),
        name="pool_mixer",
    )(x, _row(gpre), w_grp.astype(BF16), _row(b_grp), _row(scale), _row(gpost))


def _xattn_tile(x, kt_ref, v_ref, wq_ref, wo_ref, gpre_ref, gpost_ref):
    d = x.shape[-1]
    hd = d // N_XA_HEADS
    h = _rms(x, gpre_ref[...]).astype(BF16)
    q = jnp.dot(h, wq_ref[...], preferred_element_type=F32) * (hd ** -0.5)
    heads = []
    for i in range(N_XA_HEADS):
        sl = slice(i * hd, (i + 1) * hd)
        s = jnp.dot(q[:, sl].astype(BF16), kt_ref[0, sl, :], preferred_element_type=F32)
        p = jnp.exp(s - jnp.max(s, axis=-1, keepdims=True))
        p = p / jnp.sum(p, axis=-1, keepdims=True)
        heads.append(jnp.dot(p.astype(BF16), v_ref[0, :, sl], preferred_element_type=F32).astype(BF16))
    y = jnp.dot(jnp.concatenate(heads, axis=-1), wo_ref[...], preferred_element_type=F32)
    return x + _rms(y, gpost_ref[...])


def _route_tile(x, gffn_ref, wr_ref, meta_ref, cnt_ref, carry_ref, n_experts):
    tm = x.shape[0]

    @pl.when(pl.program_id(0) == 0)
    def _():
        carry_ref[...] = jnp.zeros_like(carry_ref)

    hn = _rms(x, gffn_ref[...]).astype(BF16)
    logits = jnp.dot(hn, wr_ref[...], preferred_element_type=F32)
    lane = lax.broadcasted_iota(jnp.int32, (tm, LANES), 1).astype(F32)
    logits = jnp.where(lane < n_experts, logits, -jnp.inf)
    m1 = jnp.max(logits, axis=-1, keepdims=True)
    e1 = jnp.min(jnp.where(logits == m1, lane, float(LANES)), axis=-1, keepdims=True)
    rest = jnp.where(lane == e1, -jnp.inf, logits)
    m2 = jnp.max(rest, axis=-1, keepdims=True)
    e2 = jnp.min(jnp.where(rest == m2, lane, float(LANES)), axis=-1, keepdims=True)
    ex = jnp.exp(m2 - m1)
    w1 = 1.0 / (1.0 + ex)
    w2 = ex / (1.0 + ex)

    hit1 = lane == e1
    hit2 = lane == e2
    onehot = jnp.where(hit1 | hit2, 1.0, 0.0)
    rows = lax.broadcasted_iota(jnp.int32, (tm, tm), 0)
    cols = lax.broadcasted_iota(jnp.int32, (tm, tm), 1)
    below = jnp.where(rows > cols, 1.0, 0.0).astype(BF16)
    before = jnp.dot(below, onehot.astype(BF16), preferred_element_type=F32) + carry_ref[0:1, :]
    r1 = jnp.sum(jnp.where(hit1, before, 0.0), axis=-1, keepdims=True)
    r2 = jnp.sum(jnp.where(hit2, before, 0.0), axis=-1, keepdims=True)
    carry_ref[...] = carry_ref[...] + jnp.sum(onehot, axis=0, keepdims=True)
    cnt_ref[...] = carry_ref[...]

    meta = jnp.zeros((tm, LANES), F32)
    for k, val in enumerate((e1, e2, r1, r2, w1, w2)):
        meta = jnp.where(lane == float(k), val, meta)
    meta_ref[...] = meta[:, :meta_ref.shape[1]]


def _xattn_kernel(x_ref, kt_ref, v_ref, wq_ref, wo_ref, gpre_ref, gpost_ref, *rest, n_experts):
    xn = _xattn_tile(x_ref[...], kt_ref, v_ref, wq_ref, wo_ref, gpre_ref, gpost_ref)
    if n_experts:
        gffn_ref, wr_ref, o_ref, meta_ref, cnt_ref, carry_ref = rest
        _route_tile(xn, gffn_ref, wr_ref, meta_ref, cnt_ref, carry_ref, n_experts)
    else:
        o_ref, = rest
    o_ref[...] = xn


def _xattn_layer(x, kt, v, wq, wo, gpre, gpost, *, seq, tm, route=None):
    n, d = x.shape
    m = kt.shape[-1]
    tps = seq // tm
    tok = pl.BlockSpec((tm, d), lambda i: (i, 0))
    in_specs = [
        tok,
        pl.BlockSpec((1, d, m), lambda i: (i // tps, 0, 0)),
        pl.BlockSpec((1, m, d), lambda i: (i // tps, 0, 0)),
        _full((d, d)), _full((d, d)), _full((1, d)), _full((1, d)),
    ]
    args = [x, kt, v, wq.astype(BF16), wo.astype(BF16), _row(gpre), _row(gpost)]
    out_specs, out_shape, scratch, n_experts = [tok], [jax.ShapeDtypeStruct((n, d), F32)], [], 0
    if route is not None:
        gffn, w_router = route
        n_experts = w_router.shape[1]
        in_specs += [_full((1, d)), _full((d, LANES))]
        args += [_row(gffn), jnp.pad(w_router, ((0, 0), (0, LANES - n_experts))).astype(BF16)]
        out_specs += [pl.BlockSpec((tm, SUBLANES), lambda i: (i, 0)), _full((SUBLANES, LANES))]
        out_shape += [jax.ShapeDtypeStruct((n, SUBLANES), F32), jax.ShapeDtypeStruct((SUBLANES, LANES), F32)]
        scratch = [pltpu.VMEM((SUBLANES, LANES), F32)]
    out = pl.pallas_call(
        functools.partial(_xattn_kernel, n_experts=n_experts),
        grid=(n // tm,),
        in_specs=in_specs,
        out_specs=out_specs,
        out_shape=out_shape,
        scratch_shapes=scratch,
        compiler_params=_params("arbitrary"),
        name="mem_xattn_route" if route is not None else "mem_xattn",
    )(*args)
    return out if route is not None else out[0]


def _swiglu_step(x_ref, gpre_ref, wg_ref, wu_ref, wd_ref, hn_ref, acc_ref):
    j = pl.program_id(1)

    @pl.when(j == 0)
    def _():
        hn_ref[...] = _rms(x_ref[...], gpre_ref[...]).astype(BF16)
        acc_ref[...] = jnp.zeros_like(acc_ref)

    hn = hn_ref[...]
    gate = jnp.dot(hn, wg_ref[...].astype(BF16), preferred_element_type=F32)
    up = jnp.dot(hn, wu_ref[...].astype(BF16), preferred_element_type=F32)
    act = (gate * jax.nn.sigmoid(gate) * up).astype(BF16)
    acc_ref[...] += jnp.dot(act, wd_ref[...].astype(BF16), preferred_element_type=F32)


def _ffn_kernel(x_ref, gpre_ref, wg_ref, wu_ref, wd_ref, gpost_ref, o_ref, hn_ref, acc_ref):
    _swiglu_step(x_ref, gpre_ref, wg_ref, wu_ref, wd_ref, hn_ref, acc_ref)

    @pl.when(pl.program_id(1) == pl.num_programs(1) - 1)
    def _():
        o_ref[...] = x_ref[...] + _rms(acc_ref[...], gpost_ref[...])


def _ffn_layer(x, gpre, w_gu, w_down, gpost, *, tm, fb):
    n, d = x.shape
    f = w_down.shape[0]
    nf = f // fb
    tok = pl.BlockSpec((tm, d), lambda i, j: (i, 0))
    return pl.pallas_call(
        _ffn_kernel,
        grid=(n // tm, nf),
        in_specs=[
            tok, _full((1, d)),
            pl.BlockSpec((d, fb), lambda i, j: (0, j)),
            pl.BlockSpec((d, fb), lambda i, j: (0, nf + j)),
            pl.BlockSpec((fb, d), lambda i, j: (j, 0)),
            _full((1, d)),
        ],
        out_specs=tok,
        out_shape=jax.ShapeDtypeStruct((n, d), F32),
        scratch_shapes=[pltpu.VMEM((tm, d), BF16), pltpu.VMEM((tm, d), F32)],
        compiler_params=_params("arbitrary", "arbitrary"),
        name="dense_swiglu",
    )(x, _row(gpre), w_gu, w_gu, w_down, _row(gpost))


def _experts_kernel(tile_expert_ref, ntiles_ref, x_ref, gpre_ref, wg_ref, wu_ref, wd_ref, y_ref, hn_ref):
    i = pl.program_id(0)
    last = pl.program_id(1) == pl.num_programs(1) - 1
    live = i < ntiles_ref[0]

    @pl.when(live)
    def _():
        _swiglu_step(x_ref, gpre_ref, wg_ref.at[0], wu_ref.at[0], wd_ref.at[0], hn_ref, y_ref)

    @pl.when(jnp.logical_not(live) & last)
    def _():
        y_ref[...] = jnp.zeros_like(y_ref)


def _experts_layer(xs, gpre, w_gu, w_down, tile_expert, ntiles, *, te, fb):
    r, d = xs.shape
    _, f, _ = w_down.shape
    nf = f // fb

    def x_map(i, j, te_ref, nt_ref):
        return (jnp.minimum(i, nt_ref[0] - 1), 0)

    def fblock(i, j, nt_ref):
        return jnp.where(i < nt_ref[0], j, nf - 1)

    def expert(i, te_ref, nt_ref):
        return te_ref[jnp.minimum(i, nt_ref[0] - 1)]

    grid_spec = pltpu.PrefetchScalarGridSpec(
        num_scalar_prefetch=2,
        grid=(r // te, nf),
        in_specs=[
            pl.BlockSpec((te, d), x_map),
            pl.BlockSpec((1, d), lambda i, j, te_ref, nt_ref: (0, 0)),
            pl.BlockSpec((1, d, fb), lambda i, j, te_ref, nt_ref: (expert(i, te_ref, nt_ref), 0, fblock(i, j, nt_ref))),
            pl.BlockSpec((1, d, fb),
                         lambda i, j, te_ref, nt_ref: (expert(i, te_ref, nt_ref), 0, nf + fblock(i, j, nt_ref))),
            pl.BlockSpec((1, fb, d), lambda i, j, te_ref, nt_ref: (expert(i, te_ref, nt_ref), fblock(i, j, nt_ref), 0)),
        ],
        out_specs=pl.BlockSpec((te, d), lambda i, j, te_ref, nt_ref: (i, 0)),
        scratch_shapes=[pltpu.VMEM((te, d), BF16)],
    )
    return pl.pallas_call(
        _experts_kernel,
        grid_spec=grid_spec,
        out_shape=jax.ShapeDtypeStruct((r, d), F32),
        compiler_params=_params("arbitrary", "arbitrary"),
        name="moe_experts",
    )(tile_expert, ntiles, xs, _row(gpre), w_gu, w_gu, w_down)


def _conv_kernel(x_ref, gpre_ref, w1_ref, b1_ref, dw_ref, dwb_ref, lng_ref, lnb_ref, w2_ref, b2_ref, gpost_ref,
                 o_ref, uext_ref, v_ref, *, tiles_per_seq, width):
    tm, d = x_ref.shape
    nblk = d // LANES
    t = pl.program_id(0) % tiles_per_seq

    @pl.when(t == 0)
    def _():
        uext_ref[:, 0:CONV_HALO, :] = jnp.zeros((nblk, CONV_HALO, LANES), F32)

    x = x_ref[...]
    h = _rms(x, gpre_ref[...]).astype(BF16)
    a = jnp.dot(h, w1_ref[...], preferred_element_type=F32) + b1_ref[...]
    u = a[:, :d] * jax.nn.sigmoid(a[:, d:])
    for c in range(nblk):
        uext_ref[c, CONV_HALO:, :] = u[:, c * LANES:(c + 1) * LANES]

    first = CONV_HALO - (width - 1)

    def lane_block(c, carry):
        taps = [dw_ref[c, k:k + 1, :] for k in range(width)]
        for r0 in range(0, tm, CONV_ROWS):
            acc = taps[0] * uext_ref[c, r0 + first:r0 + first + CONV_ROWS, :]
            for k in range(1, width):
                acc = acc + taps[k] * uext_ref[c, r0 + first + k:r0 + first + k + CONV_ROWS, :]
            v_ref[c, r0:r0 + CONV_ROWS, :] = acc
        uext_ref[c, 0:CONV_HALO, :] = uext_ref[c, tm:tm + CONV_HALO, :]
        return carry

    lax.fori_loop(0, nblk, lane_block, 0)

    v = jnp.concatenate([v_ref[c] for c in range(nblk)], axis=-1) + dwb_ref[...]
    mu = jnp.mean(v, axis=-1, keepdims=True)
    vc = v - mu
    var = jnp.mean(vc * vc, axis=-1, keepdims=True)
    z = vc * lax.rsqrt(var + EPS) * lng_ref[...] + lnb_ref[...]
    z = (z * jax.nn.sigmoid(z)).astype(BF16)
    y = jnp.dot(z, w2_ref[...], preferred_element_type=F32) + b2_ref[...]
    o_ref[...] = x + _rms(y, gpost_ref[...])


def _conv_layer(x, gpre, pw1_w, pw1_b, dw_w, dw_b, ln_g, ln_b, pw2_w, pw2_b, gpost, *, seq, tm):
    n, d = x.shape
    width = dw_w.shape[0]
    nblk = d // LANES
    kpad = -(-width // SUBLANES) * SUBLANES
    dw = jnp.pad(dw_w, ((0, kpad - width), (0, 0))).reshape(kpad, nblk, LANES).transpose(1, 0, 2)
    tok = pl.BlockSpec((tm, d), lambda i: (i, 0))
    vec = _full((1, d))
    return pl.pallas_call(
        functools.partial(_conv_kernel, tiles_per_seq=seq // tm, width=width),
        grid=(n // tm,),
        in_specs=[tok, vec, _full((d, 2 * d)), _full((1, 2 * d)), _full((nblk, kpad, LANES)),
                  vec, vec, vec, _full((d, d)), vec, vec],
        out_specs=tok,
        out_shape=jax.ShapeDtypeStruct((n, d), F32),
        scratch_shapes=[pltpu.VMEM((nblk, tm + CONV_HALO, LANES), F32), pltpu.VMEM((nblk, tm, LANES), F32)],
        compiler_params=_params("arbitrary"),
        name="conv_module",
    )(x, _row(gpre), pw1_w.astype(BF16), _row(pw1_b), dw, _row(dw_b), _row(ln_g), _row(ln_b),
      pw2_w.astype(BF16), _row(pw2_b), _row(gpost))


def _dispatch_kernel(dest_ref, pad_start_ref, pad_len_ref, x_ref, xs_ref, zero_ref, sem, zsem):
    tm, d = x_ref.shape
    i = pl.program_id(0)

    def row_copy(r, k):
        return pltpu.make_async_copy(x_ref.at[pl.ds(r, 1), :], xs_ref.at[pl.ds(dest_ref[0, 0, k * tm + r], 1), :], sem)

    def start(r, carry):
        for k in range(TOP_K):
            row_copy(r, k).start(priority=k % DMA_PRIORITIES)
        return carry

    lax.fori_loop(0, tm, start, 0, unroll=DMA_UNROLL)

    @pl.when(i == pl.num_programs(0) - 1)
    def _():
        zero_ref[...] = jnp.zeros_like(zero_ref)
        zrows = zero_ref.shape[0]
        n_experts = pad_start_ref.shape[0] - 1
        for e in range(n_experts + 1):
            def zero_copy(r, e=e):
                if e < n_experts:
                    return pltpu.make_async_copy(zero_ref.at[pl.ds(0, 1), :],
                                                 xs_ref.at[pl.ds(pad_start_ref[e] + r, 1), :], zsem)
                row0 = pl.multiple_of(pad_start_ref[e] + r * zrows, zrows)
                return pltpu.make_async_copy(zero_ref, xs_ref.at[pl.ds(row0, zrows), :], zsem)

            lax.fori_loop(0, pad_len_ref[e], lambda r, c: (zero_copy(r).start(), c)[1], 0)
            lax.fori_loop(0, pad_len_ref[e], lambda r, c: (zero_copy(r).wait(), c)[1], 0)

    def wait(r, carry):
        for k in range(TOP_K):
            row_copy(r, k).wait()
        return carry

    lax.fori_loop(0, tm, wait, 0, unroll=DMA_UNROLL)


def _dispatch(x, dest, pad_start, pad_len, *, rows, tm, zrows):
    n, d = x.shape
    return pl.pallas_call(
        _dispatch_kernel,
        grid=(n // tm,),
        in_specs=[
            pl.BlockSpec((1, 1, TOP_K * tm), lambda i: (i, 0, 0), memory_space=pltpu.SMEM),
            pl.BlockSpec(memory_space=pltpu.SMEM),
            pl.BlockSpec(memory_space=pltpu.SMEM),
            pl.BlockSpec((tm, d), lambda i: (i, 0)),
        ],
        out_specs=pl.BlockSpec(memory_space=pl.ANY),
        out_shape=jax.ShapeDtypeStruct((rows, d), F32),
        scratch_shapes=[pltpu.VMEM((zrows, d), F32), pltpu.SemaphoreType.DMA, pltpu.SemaphoreType.DMA],
        compiler_params=_params("arbitrary"),
        name="moe_dispatch",
    )(dest, pad_start, pad_len, x)


def _combine_kernel(dest_ref, x_ref, meta_ref, ys_ref, gpost_ref, o_ref, buf_ref, sem):
    tm, d = x_ref.shape

    def row_copy(r, k):
        return pltpu.make_async_copy(ys_ref.at[pl.ds(dest_ref[0, 0, k * tm + r], 1), :], buf_ref.at[k, pl.ds(r, 1), :], sem)

    def start(r, carry):
        for k in range(TOP_K):
            row_copy(r, k).start(priority=k % DMA_PRIORITIES)
        return carry

    def wait(r, carry):
        for k in range(TOP_K):
            row_copy(r, k).wait()
        return carry

    lax.fori_loop(0, tm, start, 0, unroll=DMA_UNROLL)
    lax.fori_loop(0, tm, wait, 0, unroll=DMA_UNROLL)
    meta = meta_ref[...]
    y = buf_ref[0] * meta[:, 4:5] + buf_ref[1] * meta[:, 5:6]
    o_ref[...] = x_ref[...] + _rms(y, gpost_ref[...])


def _combine(x, dest, meta, ys, gpost, *, tm):
    n, d = x.shape
    tok = pl.BlockSpec((tm, d), lambda i: (i, 0))
    return pl.pallas_call(
        _combine_kernel,
        grid=(n // tm,),
        in_specs=[
            pl.BlockSpec((1, 1, TOP_K * tm), lambda i: (i, 0, 0), memory_space=pltpu.SMEM),
            tok,
            pl.BlockSpec((tm, SUBLANES), lambda i: (i, 0)),
            pl.BlockSpec(memory_space=pl.ANY),
            _full((1, d)),
        ],
        out_specs=tok,
        out_shape=jax.ShapeDtypeStruct((n, d), F32),
        scratch_shapes=[pltpu.VMEM((TOP_K, tm, d), F32), pltpu.SemaphoreType.DMA],
        compiler_params=_params("arbitrary"),
        name="moe_combine",
    )(dest, x, meta, ys, _row(gpost))


def _moe_layer(x, meta, cnt, gpre, w_gu, w_down, gpost, *, tm, te, fb):
    n, d = x.shape
    e = w_gu.shape[0]

    counts = cnt[0, :e].astype(jnp.int32)
    padded = (counts + te - 1) // te * te
    ends = jnp.cumsum(padded)
    starts = ends - padded
    ids = meta[:, 0:TOP_K].astype(jnp.int32)
    ranks = meta[:, TOP_K:2 * TOP_K].astype(jnp.int32)
    dest = starts[ids] + ranks
    dest = dest.reshape(n // tm, tm, TOP_K).transpose(0, 2, 1).reshape(n // tm, 1, TOP_K * tm)
    rows = TOP_K * n + e * te
    ntiles = (ends[-1] // te).reshape(1)
    tile_expert = jnp.minimum(jnp.searchsorted(ends // te, jnp.arange(rows // te, dtype=jnp.int32), side="right"),
                              e - 1).astype(jnp.int32)

    zrows = min(te, ZERO_ROWS)
    assert te % zrows == 0 and rows % te == 0
    fill_start = jnp.concatenate([starts + counts, ends[-1:]])
    fill_len = jnp.concatenate([padded - counts, (rows - ends[-1:]) // zrows])
    xs = _dispatch(x, dest, fill_start, fill_len, rows=rows, tm=tm, zrows=zrows)
    ys = _experts_layer(xs, gpre, w_gu, w_down, tile_expert, ntiles, te=te, fb=fb)
    return _combine(x, dest, meta, ys, gpost, tm=tm)


def _tile(n, want):
    t = min(n, want)
    while n % t:
        t //= 2
    return t


@jax.jit
def kernel(x, mem, mem_norm_g, mix_pre_g, mix_post_g, xa_pre_g, xa_post_g, ffn_pre_g, ffn_post_g, xa_wq, xa_wkv, xa_wo, pool_w, pool_b, pool_scale, conv_pw1_w, conv_pw1_b, conv_dw_w, conv_dw_b, conv_ln_g, conv_ln_b, conv_pw2_w, conv_pw2_b, ffn_w_gu, ffn_w_down, moe_router, moe_w_gu, moe_w_down):
    b, s, d = x.shape
    depth = mix_pre_g.shape[0]
    f = ffn_w_down.shape[1]
    tm = _tile(s, 512)
    tf = _tile(s, 1024)
    fb = _tile(f, 512)
    kt, v = _memory_kv(mem, mem_norm_g, xa_wkv)
    h = x.reshape(b * s, d)
    for i in range(depth):
        j = i // 2
        if i % 2 == 0:
            h = _pool_layer(h, mix_pre_g[i], pool_w[j], pool_b[j], pool_scale[j], mix_post_g[i], seq=s, tm=tm)
        else:
            h = _conv_layer(h, mix_pre_g[i], conv_pw1_w[j], conv_pw1_b[j], conv_dw_w[j], conv_dw_b[j], conv_ln_g[j],
                            conv_ln_b[j], conv_pw2_w[j], conv_pw2_b[j], mix_post_g[i], seq=s, tm=tm)
        xattn = functools.partial(_xattn_layer, h, kt[i], v[i], xa_wq[i], xa_wo[i], xa_pre_g[i], xa_post_g[i],
                                  seq=s, tm=tm)
        if i % 2 == 0:
            h = _ffn_layer(xattn(), ffn_pre_g[i], ffn_w_gu[j], ffn_w_down[j], ffn_post_g[i], tm=tf, fb=fb)
        else:
            h, meta, cnt = xattn(route=(ffn_pre_g[i], moe_router[j]))
            h = _moe_layer(h, meta, cnt, ffn_pre_g[i], moe_w_gu[j], moe_w_down[j], ffn_post_g[i], tm=tm, te=tf, fb=fb)
    return h.reshape(b, s, d)
```

```python
import functools

import jax
import jax.numpy as jnp
from jax import lax
from jax.experimental import pallas as pl
from jax.experimental.pallas import tpu as pltpu

EPS = 1e-6
N_XA_HEADS = 4
POOL_WINDOWS = (2, 4, 8, 16)
TOP_K = 2

LANES = 128
SUBLANES = 8
POOL_HALO = 16
CONV_HALO = 32
CONV_ROWS = 64
ROW_ALIGN = SUBLANES
ZERO_ROWS = 256
VMEM_LIMIT = 56 * 1024 * 1024

F32 = jnp.float32
BF16 = jnp.bfloat16


def _rms(x, g):
    return x * lax.rsqrt(jnp.mean(x * x, axis=-1, keepdims=True) + EPS) * g


def _params(*sem):
    return pltpu.CompilerParams(dimension_semantics=sem, vmem_limit_bytes=VMEM_LIMIT)


def _row(v):
    return v.reshape(1, -1).astype(F32)


def _full(shape):
    nd = len(shape)
    return pl.BlockSpec(shape, lambda *_: (0,) * nd)


def _kv_kernel(mem_ref, g_ref, wkv_ref, kt_ref, v_ref):
    d = mem_ref.shape[-1]
    mn = _rms(mem_ref[0], g_ref[...]).astype(BF16)
    kv = jnp.dot(mn, wkv_ref[0], preferred_element_type=F32)
    kt_ref[0, 0] = kv[:, :d].T.astype(BF16)
    v_ref[0, 0] = kv[:, d:].astype(BF16)


def _memory_kv(mem, mem_norm_g, wkv):
    depth, d, _ = wkv.shape
    b, m, _ = mem.shape
    return pl.pallas_call(
        _kv_kernel,
        grid=(depth, b),
        in_specs=[
            pl.BlockSpec((1, m, d), lambda l, i: (i, 0, 0)),
            _full((1, d)),
            pl.BlockSpec((1, d, 2 * d), lambda l, i: (l, 0, 0)),
        ],
        out_specs=[
            pl.BlockSpec((1, 1, d, m), lambda l, i: (l, i, 0, 0)),
            pl.BlockSpec((1, 1, m, d), lambda l, i: (l, i, 0, 0)),
        ],
        out_shape=[
            jax.ShapeDtypeStruct((depth, b, d, m), BF16),
            jax.ShapeDtypeStruct((depth, b, m, d), BF16),
        ],
        compiler_params=_params("arbitrary", "arbitrary"),
        name="memory_kv",
    )(mem, _row(mem_norm_g), wkv.astype(BF16))


def _pool_tile(x, gpre_ref, w_ref, b_ref, scale_ref, gpost_ref, ext_ref, tiles_per_seq):
    tm, d = x.shape
    c = d // len(POOL_WINDOWS)
    t = pl.program_id(0) % tiles_per_seq

    @pl.when(t == 0)
    def _():
        ext_ref[0:POOL_HALO, :] = jnp.zeros((POOL_HALO, d), F32)

    hn = _rms(x, gpre_ref[...])
    ext_ref[POOL_HALO:, :] = hn

    pos = (t * tm + lax.broadcasted_iota(jnp.int32, (tm, 1), 0) + 1).astype(F32)
    ys = []
    for g, w in enumerate(POOL_WINDOWS):
        s = ext_ref[:, g * c:(g + 1) * c]
        span = 1
        while span < w:
            s = s + pltpu.roll(s, span, 0)
            span *= 2
        pooled = s[POOL_HALO:, :] / jnp.minimum(pos, float(w)) - hn[:, g * c:(g + 1) * c]
        y = jnp.dot(pooled.astype(BF16), w_ref[g], preferred_element_type=F32)
        ys.append(y + b_ref[:, g * c:(g + 1) * c])
    y = jnp.concatenate(ys, axis=-1) * scale_ref[...]
    ext_ref[0:POOL_HALO, :] = hn[tm - POOL_HALO:, :]
    return x + _rms(y, gpost_ref[...])


def _xattn_tile(x, kt_ref, v_ref, wq_ref, wo_ref, gpre_ref, gpost_ref):
    d = x.shape[-1]
    hd = d // N_XA_HEADS
    h = _rms(x, gpre_ref[...]).astype(BF16)
    q = jnp.dot(h, wq_ref[...], preferred_element_type=F32) * (hd ** -0.5)
    heads = []
    for i in range(N_XA_HEADS):
        sl = slice(i * hd, (i + 1) * hd)
        s = jnp.dot(q[:, sl].astype(BF16), kt_ref[0, sl, :], preferred_element_type=F32)
        p = jnp.exp(s - jnp.max(s, axis=-1, keepdims=True))
        p = p / jnp.sum(p, axis=-1, keepdims=True)
        heads.append(jnp.dot(p.astype(BF16), v_ref[0, :, sl], preferred_element_type=F32).astype(BF16))
    y = jnp.dot(jnp.concatenate(heads, axis=-1), wo_ref[...], preferred_element_type=F32)
    return x + _rms(y, gpost_ref[...])


def _route_tile(x, gffn_ref, wr_ref, meta_ref, cnt_ref, upper_ref, n_experts):
    tm = x.shape[0]

    @pl.when(pl.program_id(0) == 0)
    def _():
        before = lax.broadcasted_iota(jnp.int32, (tm, tm), 0) < lax.broadcasted_iota(jnp.int32, (tm, tm), 1)
        upper_ref[...] = jnp.where(before, 1.0, 0.0).astype(BF16)

    hn = _rms(x, gffn_ref[...]).astype(BF16)
    logits = jnp.dot(hn, wr_ref[...], preferred_element_type=F32)
    lg = logits.T[0:SUBLANES, :]
    sub = lax.broadcasted_iota(jnp.int32, (SUBLANES, tm), 0).astype(F32)
    lg = jnp.where(sub < n_experts, lg, -jnp.inf)
    m1 = jnp.max(lg, axis=0, keepdims=True)
    e1 = jnp.min(jnp.where(lg == m1, sub, float(SUBLANES)), axis=0, keepdims=True)
    rest = jnp.where(sub == e1, -jnp.inf, lg)
    m2 = jnp.max(rest, axis=0, keepdims=True)
    e2 = jnp.min(jnp.where(rest == m2, sub, float(SUBLANES)), axis=0, keepdims=True)
    ex = jnp.exp(m2 - m1)
    w1 = 1.0 / (1.0 + ex)
    w2 = ex / (1.0 + ex)

    hit1 = sub == e1
    hit2 = sub == e2
    onehot = jnp.where(hit1 | hit2, 1.0, 0.0)
    earlier = jnp.dot(onehot.astype(BF16), upper_ref[...], preferred_element_type=F32)
    r1 = jnp.sum(jnp.where(hit1, earlier, 0.0), axis=0, keepdims=True)
    r2 = jnp.sum(jnp.where(hit2, earlier, 0.0), axis=0, keepdims=True)
    cnt_ref[...] = jnp.broadcast_to(jnp.sum(onehot, axis=1, keepdims=True), cnt_ref.shape)

    rows = jnp.zeros((SUBLANES, tm), F32)
    for k, val in enumerate((e1, e2, r1, r2, w1, w2)):
        rows = jnp.where(sub == float(k), val, rows)
    cols = jnp.concatenate([rows, jnp.zeros((LANES - SUBLANES, tm), F32)], axis=0).T
    meta_ref[...] = cols[:, :meta_ref.shape[1]]


def _xattn_kernel(*refs, tiles_per_seq, pool, n_experts):
    it = iter(refs)

    def take(k):
        return [next(it) for _ in range(k)]

    (x_ref,) = take(1)
    pool_in = take(5) if pool else None
    attn_in = take(6)
    route_in = take(2) if n_experts else None
    (o_ref,) = take(1)
    route_out = take(2) if n_experts else None
    (ext_ref,) = take(1) if pool else (None,)
    (upper_ref,) = take(1) if n_experts else (None,)

    x = x_ref[...]
    if pool:
        x = _pool_tile(x, *pool_in, ext_ref, tiles_per_seq)
    xn = _xattn_tile(x, *attn_in)
    if n_experts:
        _route_tile(xn, *route_in, *route_out, upper_ref, n_experts)
    o_ref[...] = xn


def _xattn_layer(x, *, kt, v, wq, wo, gpre, gpost, seq, tm, pool=None, route=None):
    n, d = x.shape
    m = kt.shape[-1]
    tps = seq // tm
    tok = pl.BlockSpec((tm, d), lambda i: (i, 0))
    vec = _full((1, d))
    in_specs, args, scratch = [tok], [x], []
    if pool is not None:
        pg, w_grp, b_grp, scale, ppost = pool
        in_specs += [vec, _full(w_grp.shape), vec, vec, vec]
        args += [_row(pg), w_grp.astype(BF16), _row(b_grp), _row(scale), _row(ppost)]
        scratch.append(pltpu.VMEM((tm + POOL_HALO, d), F32))
    in_specs += [pl.BlockSpec((1, d, m), lambda i: (i // tps, 0, 0)), pl.BlockSpec((1, m, d), lambda i: (i // tps, 0, 0)),
                 _full((d, d)), _full((d, d)), vec, vec]
    args += [kt, v, wq.astype(BF16), wo.astype(BF16), _row(gpre), _row(gpost)]
    out_specs, out_shape, n_experts = [tok], [jax.ShapeDtypeStruct((n, d), F32)], 0
    if route is not None:
        gffn, w_router = route
        n_experts = w_router.shape[1]
        assert n_experts <= SUBLANES
        in_specs += [vec, _full((d, LANES))]
        args += [_row(gffn), jnp.pad(w_router, ((0, 0), (0, LANES - n_experts))).astype(BF16)]
        out_specs += [pl.BlockSpec((tm, SUBLANES), lambda i: (i, 0)), pl.BlockSpec((SUBLANES, LANES), lambda i: (i, 0))]
        out_shape += [jax.ShapeDtypeStruct((n, SUBLANES), F32), jax.ShapeDtypeStruct((n // tm * SUBLANES, LANES), F32)]
        scratch.append(pltpu.VMEM((tm, tm), BF16))
    out = pl.pallas_call(
        functools.partial(_xattn_kernel, tiles_per_seq=tps, pool=pool is not None, n_experts=n_experts),
        grid=(n // tm,),
        in_specs=in_specs,
        out_specs=out_specs,
        out_shape=out_shape,
        scratch_shapes=scratch,
        compiler_params=_params("arbitrary"),
        name="xattn" + ("_pool" if pool is not None else "") + ("_route" if route is not None else ""),
    )(*args)
    return out if route is not None else out[0]


def _swiglu_step(x_ref, gpre_ref, wg_ref, wu_ref, wd_ref, hn_ref, acc_ref):
    j = pl.program_id(1)

    @pl.when(j == 0)
    def _():
        x = x_ref[...]
        hn_ref[...] = (x if gpre_ref is None else _rms(x, gpre_ref[...])).astype(BF16)
        acc_ref[...] = jnp.zeros_like(acc_ref)

    hn = hn_ref[...]
    gate = jnp.dot(hn, wg_ref[...].astype(BF16), preferred_element_type=F32)
    up = jnp.dot(hn, wu_ref[...].astype(BF16), preferred_element_type=F32)
    act = (gate * jax.nn.sigmoid(gate) * up).astype(BF16)
    acc_ref[...] += jnp.dot(act, wd_ref[...].astype(BF16), preferred_element_type=F32)


def _ffn_kernel(x_ref, gpre_ref, wg_ref, wu_ref, wd_ref, gpost_ref, o_ref, hn_ref, acc_ref):
    _swiglu_step(x_ref, gpre_ref, wg_ref, wu_ref, wd_ref, hn_ref, acc_ref)

    @pl.when(pl.program_id(1) == pl.num_programs(1) - 1)
    def _():
        o_ref[...] = x_ref[...] + _rms(acc_ref[...], gpost_ref[...])


def _ffn_layer(x, gpre, w_gu, w_down, gpost, *, tm, fb):
    n, d = x.shape
    f = w_down.shape[0]
    nf = f // fb
    tok = pl.BlockSpec((tm, d), lambda i, j: (i, 0))
    return pl.pallas_call(
        _ffn_kernel,
        grid=(n // tm, nf),
        in_specs=[
            tok, _full((1, d)),
            pl.BlockSpec((d, fb), lambda i, j: (0, j)),
            pl.BlockSpec((d, fb), lambda i, j: (0, nf + j)),
            pl.BlockSpec((fb, d), lambda i, j: (j, 0)),
            _full((1, d)),
        ],
        out_specs=tok,
        out_shape=jax.ShapeDtypeStruct((n, d), F32),
        scratch_shapes=[pltpu.VMEM((tm, d), BF16), pltpu.VMEM((tm, d), F32)],
        compiler_params=_params("arbitrary", "arbitrary"),
        name="dense_swiglu",
    )(x, _row(gpre), w_gu, w_gu, w_down, _row(gpost))


def _experts_kernel(tile_expert_ref, ntiles_ref, x_ref, wg_ref, wu_ref, wd_ref, y_ref, hn_ref):
    i = pl.program_id(0)
    last = pl.program_id(1) == pl.num_programs(1) - 1
    live = i < ntiles_ref[0]

    @pl.when(live)
    def _():
        _swiglu_step(x_ref, None, wg_ref.at[0], wu_ref.at[0], wd_ref.at[0], hn_ref, y_ref)

    @pl.when(jnp.logical_not(live) & last)
    def _():
        y_ref[...] = jnp.zeros_like(y_ref)


def _experts_layer(xs, w_gu, w_down, tile_expert, ntiles, *, te, fb):
    r, d = xs.shape
    _, f, _ = w_down.shape
    nf = f // fb

    def x_map(i, j, te_ref, nt_ref):
        return (jnp.minimum(i, nt_ref[0] - 1), 0)

    def fblock(i, j, nt_ref):
        return jnp.where(i < nt_ref[0], j, nf - 1)

    def expert(i, te_ref, nt_ref):
        return te_ref[jnp.minimum(i, nt_ref[0] - 1)]

    grid_spec = pltpu.PrefetchScalarGridSpec(
        num_scalar_prefetch=2,
        grid=(r // te, nf),
        in_specs=[
            pl.BlockSpec((te, d), x_map),
            pl.BlockSpec((1, d, fb), lambda i, j, te_ref, nt_ref: (expert(i, te_ref, nt_ref), 0, fblock(i, j, nt_ref))),
            pl.BlockSpec((1, d, fb),
                         lambda i, j, te_ref, nt_ref: (expert(i, te_ref, nt_ref), 0, nf + fblock(i, j, nt_ref))),
            pl.BlockSpec((1, fb, d), lambda i, j, te_ref, nt_ref: (expert(i, te_ref, nt_ref), fblock(i, j, nt_ref), 0)),
        ],
        out_specs=pl.BlockSpec((te, d), lambda i, j, te_ref, nt_ref: (i, 0)),
        scratch_shapes=[pltpu.VMEM((te, d), BF16)],
    )
    return pl.pallas_call(
        _experts_kernel,
        grid_spec=grid_spec,
        out_shape=jax.ShapeDtypeStruct((r, d), F32),
        compiler_params=_params("arbitrary", "arbitrary"),
        name="moe_experts",
    )(tile_expert, ntiles, xs, w_gu, w_gu, w_down)


def _conv_kernel(x_ref, gpre_ref, w1_ref, b1_ref, dw_ref, dwb_ref, lng_ref, lnb_ref, w2_ref, b2_ref, gpost_ref,
                 o_ref, uext_ref, v_ref, *, tiles_per_seq, width):
    tm, d = x_ref.shape
    nblk = d // LANES
    t = pl.program_id(0) % tiles_per_seq

    @pl.when(t == 0)
    def _():
        uext_ref[:, 0:CONV_HALO, :] = jnp.zeros((nblk, CONV_HALO, LANES), F32)

    x = x_ref[...]
    h = _rms(x, gpre_ref[...]).astype(BF16)
    a = jnp.dot(h, w1_ref[...], preferred_element_type=F32) + b1_ref[...]
    u = a[:, :d] * jax.nn.sigmoid(a[:, d:])
    for c in range(nblk):
        uext_ref[c, CONV_HALO:, :] = u[:, c * LANES:(c + 1) * LANES]

    first = CONV_HALO - (width - 1)

    def lane_block(c, carry):
        taps = [dw_ref[c, k:k + 1, :] for k in range(width)]
        for r0 in range(0, tm, CONV_ROWS):
            acc = taps[0] * uext_ref[c, r0 + first:r0 + first + CONV_ROWS, :]
            for k in range(1, width):
                acc = acc + taps[k] * uext_ref[c, r0 + first + k:r0 + first + k + CONV_ROWS, :]
            v_ref[c, r0:r0 + CONV_ROWS, :] = acc
        uext_ref[c, 0:CONV_HALO, :] = uext_ref[c, tm:tm + CONV_HALO, :]
        return carry

    lax.fori_loop(0, nblk, lane_block, 0)

    v = jnp.concatenate([v_ref[c] for c in range(nblk)], axis=-1) + dwb_ref[...]
    mu = jnp.mean(v, axis=-1, keepdims=True)
    vc = v - mu
    var = jnp.mean(vc * vc, axis=-1, keepdims=True)
    z = vc * lax.rsqrt(var + EPS) * lng_ref[...] + lnb_ref[...]
    z = (z * jax.nn.sigmoid(z)).astype(BF16)
    y = jnp.dot(z, w2_ref[...], preferred_element_type=F32) + b2_ref[...]
    o_ref[...] = x + _rms(y, gpost_ref[...])


def _conv_layer(x, gpre, pw1_w, pw1_b, dw_w, dw_b, ln_g, ln_b, pw2_w, pw2_b, gpost, *, seq, tm):
    n, d = x.shape
    width = dw_w.shape[0]
    nblk = d // LANES
    kpad = -(-width // SUBLANES) * SUBLANES
    dw = jnp.pad(dw_w, ((0, kpad - width), (0, 0))).reshape(kpad, nblk, LANES).transpose(1, 0, 2)
    tok = pl.BlockSpec((tm, d), lambda i: (i, 0))
    vec = _full((1, d))
    return pl.pallas_call(
        functools.partial(_conv_kernel, tiles_per_seq=seq // tm, width=width),
        grid=(n // tm,),
        in_specs=[tok, vec, _full((d, 2 * d)), _full((1, 2 * d)), _full((nblk, kpad, LANES)),
                  vec, vec, vec, _full((d, d)), vec, vec],
        out_specs=tok,
        out_shape=jax.ShapeDtypeStruct((n, d), F32),
        scratch_shapes=[pltpu.VMEM((nblk, tm + CONV_HALO, LANES), F32), pltpu.VMEM((nblk, tm, LANES), F32)],
        compiler_params=_params("arbitrary"),
        name="conv_module",
    )(x, _row(gpre), pw1_w.astype(BF16), _row(pw1_b), dw, _row(dw_b), _row(ln_g), _row(ln_b),
      pw2_w.astype(BF16), _row(pw2_b), _row(gpost))


def _chunk_pieces(tbl_ref, n_experts, max_piece, visit):
    for e in range(n_experts):
        base, length, off = tbl_ref[0, 0, e], tbl_ref[0, 0, n_experts + e], tbl_ref[0, 0, 2 * n_experts + e]
        p = max_piece
        while p >= ROW_ALIGN:
            done = jnp.bitwise_and(length, ~(2 * p - 1))

            @pl.when(jnp.bitwise_and(length, p) != 0)
            def _(p=p, done=done):
                visit(p, pl.multiple_of(off + done, ROW_ALIGN), pl.multiple_of(base + done, ROW_ALIGN))

            p //= 2


def _dispatch_kernel(tbl_ref, fill_start_ref, fill_len_ref, x_ref, gpre_ref, lrow_ref, xs_ref, xp_ref, zero_ref,
                     sem, zsem, *, n_experts):
    tm, d = x_ref.shape
    rows = xp_ref.shape[0]
    i = pl.program_id(0)

    hn = _rms(x_ref[...], gpre_ref[...]).astype(BF16)
    row_id = lax.broadcasted_iota(jnp.int32, (rows, tm), 0)
    hit = (row_id == lrow_ref[0, 0:1, :]) | (row_id == lrow_ref[0, 1:2, :])
    perm = jnp.where(hit, 1.0, 0.0).astype(BF16)
    xp_ref[...] = jnp.dot(perm, hn, preferred_element_type=F32)

    def chunk_copy(p, tile_row, global_row):
        return pltpu.make_async_copy(xp_ref.at[pl.ds(tile_row, p), :], xs_ref.at[pl.ds(global_row, p), :], sem)

    _chunk_pieces(tbl_ref, n_experts, tm, lambda *a: chunk_copy(*a).start())

    @pl.when(i == pl.num_programs(0) - 1)
    def _():
        zero_ref[...] = jnp.zeros_like(zero_ref)
        zrows = zero_ref.shape[0]
        for e in range(n_experts + 1):
            step = ROW_ALIGN if e < n_experts else zrows

            def zero_copy(r, e=e, step=step):
                row0 = pl.multiple_of(fill_start_ref[e] + r * step, step)
                return pltpu.make_async_copy(zero_ref.at[pl.ds(0, step), :], xs_ref.at[pl.ds(row0, step), :], zsem)

            lax.fori_loop(0, fill_len_ref[e], lambda r, c: (zero_copy(r).start(), c)[1], 0)
            lax.fori_loop(0, fill_len_ref[e], lambda r, c: (zero_copy(r).wait(), c)[1], 0)

    _chunk_pieces(tbl_ref, n_experts, tm, lambda *a: chunk_copy(*a).wait())


def _dispatch(x, gpre, tbl, lrow, fill_start, fill_len, *, rows, tm, tile_rows, zrows, n_experts):
    n, d = x.shape
    return pl.pallas_call(
        functools.partial(_dispatch_kernel, n_experts=n_experts),
        grid=(n // tm,),
        in_specs=[
            pl.BlockSpec((1, 1, tbl.shape[2]), lambda i: (i, 0, 0), memory_space=pltpu.SMEM),
            pl.BlockSpec(memory_space=pltpu.SMEM),
            pl.BlockSpec(memory_space=pltpu.SMEM),
            pl.BlockSpec((tm, d), lambda i: (i, 0)),
            _full((1, d)),
            pl.BlockSpec((1, SUBLANES, tm), lambda i: (i, 0, 0)),
        ],
        out_specs=pl.BlockSpec(memory_space=pl.ANY),
        out_shape=jax.ShapeDtypeStruct((rows, d), F32),
        scratch_shapes=[pltpu.VMEM((tile_rows, d), F32), pltpu.VMEM((zrows, d), F32),
                        pltpu.SemaphoreType.DMA, pltpu.SemaphoreType.DMA],
        compiler_params=_params("arbitrary"),
        name="moe_dispatch",
    )(tbl, fill_start, fill_len, x, _row(gpre), lrow)


def _combine_kernel(tbl_ref, x_ref, meta_ref, lcol_ref, ys_ref, gpost_ref, o_ref, yp_ref, sem, *, n_experts):
    tm, d = x_ref.shape
    rows = yp_ref.shape[0]

    @pl.when(pl.program_id(0) == 0)
    def _():
        yp_ref[...] = jnp.zeros_like(yp_ref)

    def chunk_copy(p, tile_row, global_row):
        return pltpu.make_async_copy(ys_ref.at[pl.ds(global_row, p), :], yp_ref.at[pl.ds(tile_row, p), :], sem)

    _chunk_pieces(tbl_ref, n_experts, tm, lambda *a: chunk_copy(*a).start())
    _chunk_pieces(tbl_ref, n_experts, tm, lambda *a: chunk_copy(*a).wait())

    yp = yp_ref[...].astype(BF16)
    row_id = lax.broadcasted_iota(jnp.int32, (tm, rows), 1)
    lcol = lcol_ref[...].astype(jnp.int32)
    meta = meta_ref[...]
    y = None
    for k in range(TOP_K):
        pick = jnp.where(row_id == lcol[:, k:k + 1], 1.0, 0.0).astype(BF16)
        yk = jnp.dot(pick, yp, preferred_element_type=F32) * meta[:, 2 * TOP_K + k:2 * TOP_K + k + 1]
        y = yk if y is None else y + yk
    o_ref[...] = x_ref[...] + _rms(y, gpost_ref[...])


def _combine(x, tbl, meta, lcol, ys, gpost, *, tm, tile_rows, n_experts):
    n, d = x.shape
    tok = pl.BlockSpec((tm, d), lambda i: (i, 0))
    return pl.pallas_call(
        functools.partial(_combine_kernel, n_experts=n_experts),
        grid=(n // tm,),
        in_specs=[
            pl.BlockSpec((1, 1, tbl.shape[2]), lambda i: (i, 0, 0), memory_space=pltpu.SMEM),
            tok,
            pl.BlockSpec((tm, SUBLANES), lambda i: (i, 0)),
            pl.BlockSpec((tm, SUBLANES), lambda i: (i, 0)),
            pl.BlockSpec(memory_space=pl.ANY),
            _full((1, d)),
        ],
        out_specs=tok,
        out_shape=jax.ShapeDtypeStruct((n, d), F32),
        scratch_shapes=[pltpu.VMEM((tile_rows, d), F32), pltpu.SemaphoreType.DMA],
        compiler_params=_params("arbitrary"),
        name="moe_combine",
    )(tbl, x, meta, lcol, ys, _row(gpost))


def _moe_layer(x, meta, cnt, gpre, w_gu, w_down, gpost, *, tm, te, fb):
    n, d = x.shape
    e = w_gu.shape[0]
    nt = n // tm

    counts = cnt.reshape(nt, SUBLANES, LANES)[:, :e, 0].astype(jnp.int32)
    chunk = (counts + ROW_ALIGN - 1) // ROW_ALIGN * ROW_ALIGN
    tile_off = jnp.cumsum(chunk, axis=1) - chunk
    total = jnp.sum(chunk, axis=0)
    padded = (total + te - 1) // te * te
    ends = jnp.cumsum(padded)
    starts = ends - padded
    base = starts[None, :] + jnp.cumsum(chunk, axis=0) - chunk
    ids = meta[:, 0:TOP_K].astype(jnp.int32).reshape(nt, tm, TOP_K)
    ranks = meta[:, TOP_K:2 * TOP_K].astype(jnp.int32).reshape(nt, tm, TOP_K)
    pick = ids[..., None] == jnp.arange(e, dtype=jnp.int32)
    lrow = jnp.sum(jnp.where(pick, tile_off[:, None, None, :], 0), axis=-1) + ranks
    lcol = jnp.pad(lrow.reshape(n, TOP_K), ((0, 0), (0, SUBLANES - TOP_K))).astype(F32)
    lrow = jnp.pad(lrow.transpose(0, 2, 1), ((0, 0), (0, SUBLANES - TOP_K), (0, 0)), constant_values=-1)
    tbl = jnp.concatenate([base, chunk, tile_off], axis=1).reshape(nt, 1, 3 * e)
    tile_rows = -(-(TOP_K * tm + e * ROW_ALIGN) // LANES) * LANES

    rows = TOP_K * n + e * ROW_ALIGN * nt + e * te
    rows = -(-rows // te) * te
    ntiles = (ends[-1] // te).reshape(1)
    tile_expert = jnp.minimum(jnp.searchsorted(ends // te, jnp.arange(rows // te, dtype=jnp.int32), side="right"),
                              e - 1).astype(jnp.int32)

    zrows = min(te, ZERO_ROWS)
    assert te % zrows == 0 and zrows % ROW_ALIGN == 0
    fill_start = jnp.concatenate([starts + total, ends[-1:]])
    fill_len = jnp.concatenate([(padded - total) // ROW_ALIGN, (rows - ends[-1:]) // zrows])
    xs = _dispatch(x, gpre, tbl, lrow, fill_start, fill_len, rows=rows, tm=tm, tile_rows=tile_rows, zrows=zrows,
                   n_experts=e)
    ys = _experts_layer(xs, w_gu, w_down, tile_expert, ntiles, te=te, fb=fb)
    return _combine(x, tbl, meta, lcol, ys, gpost, tm=tm, tile_rows=tile_rows, n_experts=e)


def _tile(n, want):
    t = min(n, want)
    while n % t:
        t //= 2
    return t


@jax.jit
def kernel(x, mem, mem_norm_g, mix_pre_g, mix_post_g, xa_pre_g, xa_post_g, ffn_pre_g, ffn_post_g, xa_wq, xa_wkv, xa_wo, pool_w, pool_b, pool_scale, conv_pw1_w, conv_pw1_b, conv_dw_w, conv_dw_b, conv_ln_g, conv_ln_b, conv_pw2_w, conv_pw2_b, ffn_w_gu, ffn_w_down, moe_router, moe_w_gu, moe_w_down):
    b, s, d = x.shape
    depth = mix_pre_g.shape[0]
    f = ffn_w_down.shape[1]
    tm = _tile(s, 512)
    tf = _tile(s, 1024)
    fb = _tile(f, 512)
    kt, v = _memory_kv(mem, mem_norm_g, xa_wkv)
    h = x.reshape(b * s, d)
    for i in range(depth):
        j = i // 2
        xattn = functools.partial(_xattn_layer, kt=kt[i], v=v[i], wq=xa_wq[i], wo=xa_wo[i], gpre=xa_pre_g[i],
                                  gpost=xa_post_g[i], seq=s, tm=tm)
        if i % 2 == 0:
            h = xattn(h, pool=(mix_pre_g[i], pool_w[j], pool_b[j], pool_scale[j], mix_post_g[i]))
            h = _ffn_layer(h, ffn_pre_g[i], ffn_w_gu[j], ffn_w_down[j], ffn_post_g[i], tm=tf, fb=fb)
        else:
            h = _conv_layer(h, mix_pre_g[i], conv_pw1_w[j], conv_pw1_b[j], conv_dw_w[j], conv_dw_b[j], conv_ln_g[j],
                            conv_ln_b[j], conv_pw2_w[j], conv_pw2_b[j], mix_post_g[i], seq=s, tm=tm)
            h, meta, cnt = xattn(h, route=(ffn_pre_g[i], moe_router[j]))
            h = _moe_layer(h, meta, cnt, ffn_pre_g[i], moe_w_gu[j], moe_w_down[j], ffn_post_g[i], tm=tm, te=tf, fb=fb)
    return h.reshape(b, s, d)
```

```python
import functools

import jax
import jax.numpy as jnp
from jax import lax
from jax.experimental import pallas as pl
from jax.experimental.pallas import tpu as pltpu

EPS = 1e-6
N_XA_HEADS = 4
POOL_WINDOWS = (2, 4, 8, 16)
TOP_K = 2

LANES = 128
SUBLANES = 8
POOL_HALO = 16
CONV_HALO = 32
CONV_ROWS = 64
ROW_ALIGN = SUBLANES
ZERO_ROWS = 256
VMEM_LIMIT = 56 * 1024 * 1024

F32 = jnp.float32
BF16 = jnp.bfloat16


def _rms(x, g):
    return x * lax.rsqrt(jnp.mean(x * x, axis=-1, keepdims=True) + EPS) * g


def _params(*sem):
    return pltpu.CompilerParams(dimension_semantics=sem, vmem_limit_bytes=VMEM_LIMIT)


def _row(v):
    return v.reshape(1, -1).astype(F32)


def _full(shape):
    nd = len(shape)
    return pl.BlockSpec(shape, lambda *_: (0,) * nd)


def _kv_kernel(mem_ref, g_ref, wkv_ref, kt_ref, v_ref):
    d = mem_ref.shape[-1]
    mn = _rms(mem_ref[0], g_ref[...]).astype(BF16)
    kv = jnp.dot(mn, wkv_ref[0], preferred_element_type=F32)
    kt_ref[0, 0] = kv[:, :d].T.astype(BF16)
    v_ref[0, 0] = kv[:, d:].astype(BF16)


def _memory_kv(mem, mem_norm_g, wkv):
    depth, d, _ = wkv.shape
    b, m, _ = mem.shape
    return pl.pallas_call(
        _kv_kernel,
        grid=(depth, b),
        in_specs=[
            pl.BlockSpec((1, m, d), lambda l, i: (i, 0, 0)),
            _full((1, d)),
            pl.BlockSpec((1, d, 2 * d), lambda l, i: (l, 0, 0)),
        ],
        out_specs=[
            pl.BlockSpec((1, 1, d, m), lambda l, i: (l, i, 0, 0)),
            pl.BlockSpec((1, 1, m, d), lambda l, i: (l, i, 0, 0)),
        ],
        out_shape=[
            jax.ShapeDtypeStruct((depth, b, d, m), BF16),
            jax.ShapeDtypeStruct((depth, b, m, d), BF16),
        ],
        compiler_params=_params("arbitrary", "arbitrary"),
        name="memory_kv",
    )(mem, _row(mem_norm_g), wkv.astype(BF16))


def _pool_tile(x, gpre_ref, w_ref, b_ref, scale_ref, gpost_ref, ext_ref, tiles_per_seq):
    tm, d = x.shape
    c = d // len(POOL_WINDOWS)
    t = pl.program_id(0) % tiles_per_seq

    @pl.when(t == 0)
    def _():
        ext_ref[0:POOL_HALO, :] = jnp.zeros((POOL_HALO, d), F32)

    hn = _rms(x, gpre_ref[...])
    ext_ref[POOL_HALO:, :] = hn

    pos = (t * tm + lax.broadcasted_iota(jnp.int32, (tm, 1), 0) + 1).astype(F32)
    ys = []
    for g, w in enumerate(POOL_WINDOWS):
        s = ext_ref[:, g * c:(g + 1) * c]
        span = 1
        while span < w:
            s = s + pltpu.roll(s, span, 0)
            span *= 2
        pooled = s[POOL_HALO:, :] / jnp.minimum(pos, float(w)) - hn[:, g * c:(g + 1) * c]
        y = jnp.dot(pooled.astype(BF16), w_ref[g], preferred_element_type=F32)
        ys.append(y + b_ref[:, g * c:(g + 1) * c])
    y = jnp.concatenate(ys, axis=-1) * scale_ref[...]
    ext_ref[0:POOL_HALO, :] = hn[tm - POOL_HALO:, :]
    return x + _rms(y, gpost_ref[...])


def _xattn_tile(x, kt_ref, v_ref, wq_ref, wo_ref, gpre_ref, gpost_ref):
    d = x.shape[-1]
    hd = d // N_XA_HEADS
    h = _rms(x, gpre_ref[...]).astype(BF16)
    q = jnp.dot(h, wq_ref[...], preferred_element_type=F32) * (hd ** -0.5)
    heads = []
    for i in range(N_XA_HEADS):
        sl = slice(i * hd, (i + 1) * hd)
        s = jnp.dot(q[:, sl].astype(BF16), kt_ref[0, sl, :], preferred_element_type=F32)
        p = jnp.exp(s - jnp.max(s, axis=-1, keepdims=True))
        p = p / jnp.sum(p, axis=-1, keepdims=True)
        heads.append(jnp.dot(p.astype(BF16), v_ref[0, :, sl], preferred_element_type=F32).astype(BF16))
    y = jnp.dot(jnp.concatenate(heads, axis=-1), wo_ref[...], preferred_element_type=F32)
    return x + _rms(y, gpost_ref[...])


def _route_tile(x, gffn_ref, wr_ref, meta_ref, metat_ref, cnt_ref, upper_ref, n_experts):
    tm = x.shape[0]

    @pl.when(pl.program_id(0) == 0)
    def _():
        before = lax.broadcasted_iota(jnp.int32, (tm, tm), 0) < lax.broadcasted_iota(jnp.int32, (tm, tm), 1)
        upper_ref[...] = jnp.where(before, 1.0, 0.0).astype(BF16)

    hn = _rms(x, gffn_ref[...]).astype(BF16)
    logits = jnp.dot(hn, wr_ref[...], preferred_element_type=F32)
    lg = logits.T[0:SUBLANES, :]
    sub = lax.broadcasted_iota(jnp.int32, (SUBLANES, tm), 0).astype(F32)
    lg = jnp.where(sub < n_experts, lg, -jnp.inf)
    m1 = jnp.max(lg, axis=0, keepdims=True)
    e1 = jnp.min(jnp.where(lg == m1, sub, float(SUBLANES)), axis=0, keepdims=True)
    rest = jnp.where(sub == e1, -jnp.inf, lg)
    m2 = jnp.max(rest, axis=0, keepdims=True)
    e2 = jnp.min(jnp.where(rest == m2, sub, float(SUBLANES)), axis=0, keepdims=True)
    ex = jnp.exp(m2 - m1)
    w1 = 1.0 / (1.0 + ex)
    w2 = ex / (1.0 + ex)

    hit1 = sub == e1
    hit2 = sub == e2
    onehot = jnp.where(hit1 | hit2, 1.0, 0.0)
    earlier = jnp.dot(onehot.astype(BF16), upper_ref[...], preferred_element_type=F32)
    counts = jnp.sum(onehot, axis=1, keepdims=True)
    cnt_ref[...] = jnp.broadcast_to(counts, cnt_ref.shape)
    chunk = jnp.ceil(counts * (1.0 / ROW_ALIGN)) * ROW_ALIGN
    lower = (lax.broadcasted_iota(jnp.int32, (SUBLANES, SUBLANES), 1)
             < lax.broadcasted_iota(jnp.int32, (SUBLANES, SUBLANES), 0))
    chunk_start = jnp.dot(jnp.where(lower, 1.0, 0.0).astype(BF16), jnp.broadcast_to(chunk, (SUBLANES, LANES)).astype(BF16),
                          preferred_element_type=F32)[:, 0:1]
    place = chunk_start + earlier
    r1 = jnp.sum(jnp.where(hit1, place, 0.0), axis=0, keepdims=True)
    r2 = jnp.sum(jnp.where(hit2, place, 0.0), axis=0, keepdims=True)

    rows = jnp.zeros((SUBLANES, tm), F32)
    for k, val in enumerate((e1, e2, r1, r2, w1, w2)):
        rows = jnp.where(sub == float(k), val, rows)
    metat_ref[...] = rows
    cols = jnp.concatenate([rows, jnp.zeros((LANES - SUBLANES, tm), F32)], axis=0).T
    meta_ref[...] = cols[:, :meta_ref.shape[1]]


def _xattn_kernel(*refs, tiles_per_seq, pool, n_experts):
    it = iter(refs)

    def take(k):
        return [next(it) for _ in range(k)]

    (x_ref,) = take(1)
    pool_in = take(5) if pool else None
    attn_in = take(6)
    route_in = take(2) if n_experts else None
    (o_ref,) = take(1)
    route_out = take(3) if n_experts else None
    (ext_ref,) = take(1) if pool else (None,)
    (upper_ref,) = take(1) if n_experts else (None,)

    x = x_ref[...]
    if pool:
        x = _pool_tile(x, *pool_in, ext_ref, tiles_per_seq)
    xn = _xattn_tile(x, *attn_in)
    if n_experts:
        _route_tile(xn, *route_in, *route_out, upper_ref, n_experts)
    o_ref[...] = xn


def _xattn_layer(x, *, kt, v, wq, wo, gpre, gpost, seq, tm, pool=None, route=None):
    n, d = x.shape
    m = kt.shape[-1]
    tps = seq // tm
    tok = pl.BlockSpec((tm, d), lambda i: (i, 0))
    vec = _full((1, d))
    in_specs, args, scratch = [tok], [x], []
    if pool is not None:
        pg, w_grp, b_grp, scale, ppost = pool
        in_specs += [vec, _full(w_grp.shape), vec, vec, vec]
        args += [_row(pg), w_grp.astype(BF16), _row(b_grp), _row(scale), _row(ppost)]
        scratch.append(pltpu.VMEM((tm + POOL_HALO, d), F32))
    in_specs += [pl.BlockSpec((1, d, m), lambda i: (i // tps, 0, 0)), pl.BlockSpec((1, m, d), lambda i: (i // tps, 0, 0)),
                 _full((d, d)), _full((d, d)), vec, vec]
    args += [kt, v, wq.astype(BF16), wo.astype(BF16), _row(gpre), _row(gpost)]
    out_specs, out_shape, n_experts = [tok], [jax.ShapeDtypeStruct((n, d), F32)], 0
    if route is not None:
        gffn, w_router = route
        n_experts = w_router.shape[1]
        assert n_experts <= SUBLANES
        in_specs += [vec, _full((d, LANES))]
        args += [_row(gffn), jnp.pad(w_router, ((0, 0), (0, LANES - n_experts))).astype(BF16)]
        out_specs += [pl.BlockSpec((tm, SUBLANES), lambda i: (i, 0)), pl.BlockSpec((SUBLANES, tm), lambda i: (i, 0)),
                      pl.BlockSpec((SUBLANES, LANES), lambda i: (i, 0))]
        out_shape += [jax.ShapeDtypeStruct((n, SUBLANES), F32), jax.ShapeDtypeStruct((n // tm * SUBLANES, tm), F32),
                      jax.ShapeDtypeStruct((n // tm * SUBLANES, LANES), F32)]
        scratch.append(pltpu.VMEM((tm, tm), BF16))
    out = pl.pallas_call(
        functools.partial(_xattn_kernel, tiles_per_seq=tps, pool=pool is not None, n_experts=n_experts),
        grid=(n // tm,),
        in_specs=in_specs,
        out_specs=out_specs,
        out_shape=out_shape,
        scratch_shapes=scratch,
        compiler_params=_params("arbitrary"),
        name="xattn" + ("_pool" if pool is not None else "") + ("_route" if route is not None else ""),
    )(*args)
    return out if route is not None else out[0]


def _swiglu_step(x_ref, gpre_ref, wg_ref, wu_ref, wd_ref, hn_ref, acc_ref):
    j = pl.program_id(1)

    @pl.when(j == 0)
    def _():
        x = x_ref[...]
        hn_ref[...] = (x if gpre_ref is None else _rms(x, gpre_ref[...])).astype(BF16)
        acc_ref[...] = jnp.zeros_like(acc_ref)

    hn = hn_ref[...]
    gate = jnp.dot(hn, wg_ref[...].astype(BF16), preferred_element_type=F32)
    up = jnp.dot(hn, wu_ref[...].astype(BF16), preferred_element_type=F32)
    act = (gate * jax.nn.sigmoid(gate) * up).astype(BF16)
    acc_ref[...] += jnp.dot(act, wd_ref[...].astype(BF16), preferred_element_type=F32)


def _ffn_kernel(x_ref, gpre_ref, wg_ref, wu_ref, wd_ref, gpost_ref, o_ref, hn_ref, acc_ref):
    _swiglu_step(x_ref, gpre_ref, wg_ref, wu_ref, wd_ref, hn_ref, acc_ref)

    @pl.when(pl.program_id(1) == pl.num_programs(1) - 1)
    def _():
        o_ref[...] = x_ref[...] + _rms(acc_ref[...], gpost_ref[...])


def _ffn_layer(x, gpre, w_gu, w_down, gpost, *, tm, fb):
    n, d = x.shape
    f = w_down.shape[0]
    nf = f // fb
    tok = pl.BlockSpec((tm, d), lambda i, j: (i, 0))
    return pl.pallas_call(
        _ffn_kernel,
        grid=(n // tm, nf),
        in_specs=[
            tok, _full((1, d)),
            pl.BlockSpec((d, fb), lambda i, j: (0, j)),
            pl.BlockSpec((d, fb), lambda i, j: (0, nf + j)),
            pl.BlockSpec((fb, d), lambda i, j: (j, 0)),
            _full((1, d)),
        ],
        out_specs=tok,
        out_shape=jax.ShapeDtypeStruct((n, d), F32),
        scratch_shapes=[pltpu.VMEM((tm, d), BF16), pltpu.VMEM((tm, d), F32)],
        compiler_params=_params("arbitrary", "arbitrary"),
        name="dense_swiglu",
    )(x, _row(gpre), w_gu, w_gu, w_down, _row(gpost))


def _experts_kernel(tile_expert_ref, ntiles_ref, x_ref, wg_ref, wu_ref, wd_ref, y_ref, hn_ref):
    i = pl.program_id(0)
    last = pl.program_id(1) == pl.num_programs(1) - 1
    live = i < ntiles_ref[0]

    @pl.when(live)
    def _():
        _swiglu_step(x_ref, None, wg_ref.at[0], wu_ref.at[0], wd_ref.at[0], hn_ref, y_ref)

    @pl.when(jnp.logical_not(live) & last)
    def _():
        y_ref[...] = jnp.zeros_like(y_ref)


def _experts_layer(xs, w_gu, w_down, tile_expert, ntiles, *, te, fb):
    r, d = xs.shape
    _, f, _ = w_down.shape
    nf = f // fb

    def x_map(i, j, te_ref, nt_ref):
        return (jnp.minimum(i, nt_ref[0] - 1), 0)

    def fblock(i, j, nt_ref):
        return jnp.where(i < nt_ref[0], j, nf - 1)

    def expert(i, te_ref, nt_ref):
        return te_ref[jnp.minimum(i, nt_ref[0] - 1)]

    grid_spec = pltpu.PrefetchScalarGridSpec(
        num_scalar_prefetch=2,
        grid=(r // te, nf),
        in_specs=[
            pl.BlockSpec((te, d), x_map),
            pl.BlockSpec((1, d, fb), lambda i, j, te_ref, nt_ref: (expert(i, te_ref, nt_ref), 0, fblock(i, j, nt_ref))),
            pl.BlockSpec((1, d, fb),
                         lambda i, j, te_ref, nt_ref: (expert(i, te_ref, nt_ref), 0, nf + fblock(i, j, nt_ref))),
            pl.BlockSpec((1, fb, d), lambda i, j, te_ref, nt_ref: (expert(i, te_ref, nt_ref), fblock(i, j, nt_ref), 0)),
        ],
        out_specs=pl.BlockSpec((te, d), lambda i, j, te_ref, nt_ref: (i, 0)),
        scratch_shapes=[pltpu.VMEM((te, d), BF16)],
    )
    return pl.pallas_call(
        _experts_kernel,
        grid_spec=grid_spec,
        out_shape=jax.ShapeDtypeStruct((r, d), F32),
        compiler_params=_params("arbitrary", "arbitrary"),
        name="moe_experts",
    )(tile_expert, ntiles, xs, w_gu, w_gu, w_down)


def _conv_kernel(x_ref, gpre_ref, w1_ref, b1_ref, dw_ref, dwb_ref, lng_ref, lnb_ref, w2_ref, b2_ref, gpost_ref,
                 o_ref, uext_ref, v_ref, *, tiles_per_seq, width):
    tm, d = x_ref.shape
    nblk = d // LANES
    t = pl.program_id(0) % tiles_per_seq

    @pl.when(t == 0)
    def _():
        uext_ref[:, 0:CONV_HALO, :] = jnp.zeros((nblk, CONV_HALO, LANES), F32)

    x = x_ref[...]
    h = _rms(x, gpre_ref[...]).astype(BF16)
    a = jnp.dot(h, w1_ref[...], preferred_element_type=F32) + b1_ref[...]
    u = a[:, :d] * jax.nn.sigmoid(a[:, d:])
    for c in range(nblk):
        uext_ref[c, CONV_HALO:, :] = u[:, c * LANES:(c + 1) * LANES]

    first = CONV_HALO - (width - 1)

    def lane_block(c, carry):
        taps = [dw_ref[c, k:k + 1, :] for k in range(width)]
        for r0 in range(0, tm, CONV_ROWS):
            acc = taps[0] * uext_ref[c, r0 + first:r0 + first + CONV_ROWS, :]
            for k in range(1, width):
                acc = acc + taps[k] * uext_ref[c, r0 + first + k:r0 + first + k + CONV_ROWS, :]
            v_ref[c, r0:r0 + CONV_ROWS, :] = acc
        uext_ref[c, 0:CONV_HALO, :] = uext_ref[c, tm:tm + CONV_HALO, :]
        return carry

    lax.fori_loop(0, nblk, lane_block, 0)

    v = jnp.concatenate([v_ref[c] for c in range(nblk)], axis=-1) + dwb_ref[...]
    mu = jnp.mean(v, axis=-1, keepdims=True)
    vc = v - mu
    var = jnp.mean(vc * vc, axis=-1, keepdims=True)
    z = vc * lax.rsqrt(var + EPS) * lng_ref[...] + lnb_ref[...]
    z = (z * jax.nn.sigmoid(z)).astype(BF16)
    y = jnp.dot(z, w2_ref[...], preferred_element_type=F32) + b2_ref[...]
    o_ref[...] = x + _rms(y, gpost_ref[...])


def _conv_layer(x, gpre, pw1_w, pw1_b, dw_w, dw_b, ln_g, ln_b, pw2_w, pw2_b, gpost, *, seq, tm):
    n, d = x.shape
    width = dw_w.shape[0]
    nblk = d // LANES
    kpad = -(-width // SUBLANES) * SUBLANES
    dw = jnp.pad(dw_w, ((0, kpad - width), (0, 0))).reshape(kpad, nblk, LANES).transpose(1, 0, 2)
    tok = pl.BlockSpec((tm, d), lambda i: (i, 0))
    vec = _full((1, d))
    return pl.pallas_call(
        functools.partial(_conv_kernel, tiles_per_seq=seq // tm, width=width),
        grid=(n // tm,),
        in_specs=[tok, vec, _full((d, 2 * d)), _full((1, 2 * d)), _full((nblk, kpad, LANES)),
                  vec, vec, vec, _full((d, d)), vec, vec],
        out_specs=tok,
        out_shape=jax.ShapeDtypeStruct((n, d), F32),
        scratch_shapes=[pltpu.VMEM((nblk, tm + CONV_HALO, LANES), F32), pltpu.VMEM((nblk, tm, LANES), F32)],
        compiler_params=_params("arbitrary"),
        name="conv_module",
    )(x, _row(gpre), pw1_w.astype(BF16), _row(pw1_b), dw, _row(dw_b), _row(ln_g), _row(ln_b),
      pw2_w.astype(BF16), _row(pw2_b), _row(gpost))


def _chunk_pieces(tbl_ref, tile, n_experts, max_piece, visit):
    for e in range(n_experts):
        base, length, off = tbl_ref[tile, e], tbl_ref[tile, n_experts + e], tbl_ref[tile, 2 * n_experts + e]
        p = max_piece
        while p >= ROW_ALIGN:
            done = jnp.bitwise_and(length, ~(2 * p - 1))

            @pl.when(jnp.bitwise_and(length, p) != 0)
            def _(p=p, done=done):
                visit(p, pl.multiple_of(off + done, ROW_ALIGN), pl.multiple_of(base + done, ROW_ALIGN))

            p //= 2


def _dispatch_kernel(tbl_ref, fill_start_ref, fill_len_ref, x_ref, gpre_ref, metat_ref, xs_ref, xp_ref, zero_ref,
                     sem, zsem, *, n_experts):
    tm, d = x_ref.shape
    rows = xp_ref.shape[1]
    i = pl.program_id(0)
    last = pl.num_programs(0) - 1
    slot = i % 2

    hn = _rms(x_ref[...], gpre_ref[...]).astype(BF16)
    row_id = lax.broadcasted_iota(jnp.int32, (rows, tm), 0)
    place = metat_ref[TOP_K:2 * TOP_K, :].astype(jnp.int32)
    hit = (row_id == place[0:1, :]) | (row_id == place[1:2, :])
    perm = jnp.where(hit, 1.0, 0.0).astype(BF16)
    xp_ref[slot] = jnp.dot(perm, hn, preferred_element_type=F32)

    def chunk_copy(buf, p, tile_row, global_row):
        return pltpu.make_async_copy(xp_ref.at[buf, pl.ds(tile_row, p), :], xs_ref.at[pl.ds(global_row, p), :],
                                     sem.at[buf])

    _chunk_pieces(tbl_ref, i, n_experts, tm, lambda *a: chunk_copy(slot, *a).start())

    @pl.when(i > 0)
    def _():
        _chunk_pieces(tbl_ref, i - 1, n_experts, tm, lambda *a: chunk_copy(1 - slot, *a).wait())

    @pl.when(i == last)
    def _():
        zero_ref[...] = jnp.zeros_like(zero_ref)
        zrows = zero_ref.shape[0]
        for e in range(n_experts + 1):
            step = ROW_ALIGN if e < n_experts else zrows

            def zero_copy(r, e=e, step=step):
                row0 = pl.multiple_of(fill_start_ref[e] + r * step, step)
                return pltpu.make_async_copy(zero_ref.at[pl.ds(0, step), :], xs_ref.at[pl.ds(row0, step), :], zsem)

            lax.fori_loop(0, fill_len_ref[e], lambda r, c: (zero_copy(r).start(), c)[1], 0)
            lax.fori_loop(0, fill_len_ref[e], lambda r, c: (zero_copy(r).wait(), c)[1], 0)
        _chunk_pieces(tbl_ref, i, n_experts, tm, lambda *a: chunk_copy(slot, *a).wait())


def _dispatch(x, gpre, tbl, metat, fill_start, fill_len, *, rows, tm, tile_rows, zrows, n_experts):
    n, d = x.shape
    smem = pl.BlockSpec(memory_space=pltpu.SMEM)
    return pl.pallas_call(
        functools.partial(_dispatch_kernel, n_experts=n_experts),
        grid=(n // tm,),
        in_specs=[smem, smem, smem, pl.BlockSpec((tm, d), lambda i: (i, 0)), _full((1, d)),
                  pl.BlockSpec((SUBLANES, tm), lambda i: (i, 0))],
        out_specs=pl.BlockSpec(memory_space=pl.ANY),
        out_shape=jax.ShapeDtypeStruct((rows, d), F32),
        scratch_shapes=[pltpu.VMEM((2, tile_rows, d), F32), pltpu.VMEM((zrows, d), F32),
                        pltpu.SemaphoreType.DMA((2,)), pltpu.SemaphoreType.DMA],
        compiler_params=_params("arbitrary"),
        name="moe_dispatch",
    )(tbl, fill_start, fill_len, x, _row(gpre), metat)


def _combine_kernel(tbl_ref, x_ref, meta_ref, ys_ref, gpost_ref, o_ref, yp_ref, sem, *, n_experts):
    tm, d = x_ref.shape
    rows = yp_ref.shape[1]
    i = pl.program_id(0)
    slot = i % 2

    def chunk_copy(buf, p, tile_row, global_row):
        return pltpu.make_async_copy(ys_ref.at[pl.ds(global_row, p), :], yp_ref.at[buf, pl.ds(tile_row, p), :],
                                     sem.at[buf])

    @pl.when(i == 0)
    def _():
        yp_ref[...] = jnp.zeros_like(yp_ref)
        _chunk_pieces(tbl_ref, i, n_experts, tm, lambda *a: chunk_copy(slot, *a).start())

    @pl.when(i + 1 < pl.num_programs(0))
    def _():
        _chunk_pieces(tbl_ref, i + 1, n_experts, tm, lambda *a: chunk_copy(1 - slot, *a).start())

    _chunk_pieces(tbl_ref, i, n_experts, tm, lambda *a: chunk_copy(slot, *a).wait())

    yp = yp_ref[slot].astype(BF16)
    row_id = lax.broadcasted_iota(jnp.int32, (tm, rows), 1)
    meta = meta_ref[...]
    place = meta[:, TOP_K:2 * TOP_K].astype(jnp.int32)
    y = None
    for k in range(TOP_K):
        pick = jnp.where(row_id == place[:, k:k + 1], 1.0, 0.0).astype(BF16)
        yk = jnp.dot(pick, yp, preferred_element_type=F32) * meta[:, 2 * TOP_K + k:2 * TOP_K + k + 1]
        y = yk if y is None else y + yk
    o_ref[...] = x_ref[...] + _rms(y, gpost_ref[...])


def _combine(x, tbl, meta, ys, gpost, *, tm, tile_rows, n_experts):
    n, d = x.shape
    tok = pl.BlockSpec((tm, d), lambda i: (i, 0))
    return pl.pallas_call(
        functools.partial(_combine_kernel, n_experts=n_experts),
        grid=(n // tm,),
        in_specs=[
            pl.BlockSpec(memory_space=pltpu.SMEM),
            tok,
            pl.BlockSpec((tm, SUBLANES), lambda i: (i, 0)),
            pl.BlockSpec(memory_space=pl.ANY),
            _full((1, d)),
        ],
        out_specs=tok,
        out_shape=jax.ShapeDtypeStruct((n, d), F32),
        scratch_shapes=[pltpu.VMEM((2, tile_rows, d), F32), pltpu.SemaphoreType.DMA((2,))],
        compiler_params=_params("arbitrary"),
        name="moe_combine",
    )(tbl, x, meta, ys, _row(gpost))


def _moe_layer(x, meta, metat, cnt, gpre, w_gu, w_down, gpost, *, tm, te, fb):
    n, d = x.shape
    e = w_gu.shape[0]
    nt = n // tm

    counts = cnt.reshape(nt, SUBLANES, LANES)[:, :e, 0].astype(jnp.int32)
    chunk = (counts + ROW_ALIGN - 1) // ROW_ALIGN * ROW_ALIGN
    tile_off = jnp.cumsum(chunk, axis=1) - chunk
    total = jnp.sum(chunk, axis=0)
    padded = (total + te - 1) // te * te
    ends = jnp.cumsum(padded)
    starts = ends - padded
    base = starts[None, :] + jnp.cumsum(chunk, axis=0) - chunk
    tbl = jnp.concatenate([base, chunk, tile_off], axis=1)
    tile_rows = -(-(TOP_K * tm + e * ROW_ALIGN) // LANES) * LANES

    rows = TOP_K * n + e * ROW_ALIGN * nt + e * te
    rows = -(-rows // te) * te
    ntiles = (ends[-1] // te).reshape(1)
    tile_expert = jnp.minimum(jnp.searchsorted(ends // te, jnp.arange(rows // te, dtype=jnp.int32), side="right"),
                              e - 1).astype(jnp.int32)

    zrows = min(te, ZERO_ROWS)
    assert te % zrows == 0 and zrows % ROW_ALIGN == 0
    fill_start = jnp.concatenate([starts + total, ends[-1:]])
    fill_len = jnp.concatenate([(padded - total) // ROW_ALIGN, (rows - ends[-1:]) // zrows])
    xs = _dispatch(x, gpre, tbl, metat, fill_start, fill_len, rows=rows, tm=tm, tile_rows=tile_rows, zrows=zrows,
                   n_experts=e)
    ys = _experts_layer(xs, w_gu, w_down, tile_expert, ntiles, te=te, fb=fb)
    return _combine(x, tbl, meta, ys, gpost, tm=tm, tile_rows=tile_rows, n_experts=e)


def _tile(n, want):
    t = min(n, want)
    while n % t:
        t //= 2
    return t


@jax.jit
def kernel(x, mem, mem_norm_g, mix_pre_g, mix_post_g, xa_pre_g, xa_post_g, ffn_pre_g, ffn_post_g, xa_wq, xa_wkv, xa_wo, pool_w, pool_b, pool_scale, conv_pw1_w, conv_pw1_b, conv_dw_w, conv_dw_b, conv_ln_g, conv_ln_b, conv_pw2_w, conv_pw2_b, ffn_w_gu, ffn_w_down, moe_router, moe_w_gu, moe_w_down):
    b, s, d = x.shape
    depth = mix_pre_g.shape[0]
    f = ffn_w_down.shape[1]
    tm = _tile(s, 512)
    tf = _tile(s, 1024)
    fb = _tile(f, 512)
    kt, v = _memory_kv(mem, mem_norm_g, xa_wkv)
    h = x.reshape(b * s, d)
    for i in range(depth):
        j = i // 2
        xattn = functools.partial(_xattn_layer, kt=kt[i], v=v[i], wq=xa_wq[i], wo=xa_wo[i], gpre=xa_pre_g[i],
                                  gpost=xa_post_g[i], seq=s, tm=tm)
        if i % 2 == 0:
            h = xattn(h, pool=(mix_pre_g[i], pool_w[j], pool_b[j], pool_scale[j], mix_post_g[i]))
            h = _ffn_layer(h, ffn_pre_g[i], ffn_w_gu[j], ffn_w_down[j], ffn_post_g[i], tm=tf, fb=fb)
        else:
            h = _conv_layer(h, mix_pre_g[i], conv_pw1_w[j], conv_pw1_b[j], conv_dw_w[j], conv_dw_b[j], conv_ln_g[j],
                            conv_ln_b[j], conv_pw2_w[j], conv_pw2_b[j], mix_post_g[i], seq=s, tm=tm)
            h, meta, metat, cnt = xattn(h, route=(ffn_pre_g[i], moe_router[j]))
            h = _moe_layer(h, meta, metat, cnt, ffn_pre_g[i], moe_w_gu[j], moe_w_down[j], ffn_post_g[i], tm=tm, te=tf, fb=fb)
    return h.reshape(b, s, d)
```

```python
import functools

import jax
import jax.numpy as jnp
from jax import lax
from jax.experimental import pallas as pl
from jax.experimental.pallas import tpu as pltpu

EPS = 1e-6
N_XA_HEADS = 4
POOL_WINDOWS = (2, 4, 8, 16)
TOP_K = 2

LANES = 128
SUBLANES = 8
POOL_HALO = 16
CONV_HALO = 32
CONV_ROWS = 64
ROW_ALIGN = SUBLANES
ZERO_ROWS = 256
VMEM_LIMIT = 56 * 1024 * 1024

F32 = jnp.float32
BF16 = jnp.bfloat16


def _rms(x, g):
    return x * lax.rsqrt(jnp.mean(x * x, axis=-1, keepdims=True) + EPS) * g


def _params(*sem):
    return pltpu.CompilerParams(dimension_semantics=sem, vmem_limit_bytes=VMEM_LIMIT)


def _row(v):
    return v.reshape(1, -1).astype(F32)


def _full(shape):
    nd = len(shape)
    return pl.BlockSpec(shape, lambda *_: (0,) * nd)


def _kv_kernel(mem_ref, g_ref, wkv_ref, kt_ref, v_ref):
    d = mem_ref.shape[-1]
    mn = _rms(mem_ref[0], g_ref[...]).astype(BF16)
    kv = jnp.dot(mn, wkv_ref[0], preferred_element_type=F32)
    kt_ref[0, 0] = kv[:, :d].T.astype(BF16)
    v_ref[0, 0] = kv[:, d:].astype(BF16)


def _memory_kv(mem, mem_norm_g, wkv):
    depth, d, _ = wkv.shape
    b, m, _ = mem.shape
    return pl.pallas_call(
        _kv_kernel,
        grid=(depth, b),
        in_specs=[
            pl.BlockSpec((1, m, d), lambda l, i: (i, 0, 0)),
            _full((1, d)),
            pl.BlockSpec((1, d, 2 * d), lambda l, i: (l, 0, 0)),
        ],
        out_specs=[
            pl.BlockSpec((1, 1, d, m), lambda l, i: (l, i, 0, 0)),
            pl.BlockSpec((1, 1, m, d), lambda l, i: (l, i, 0, 0)),
        ],
        out_shape=[
            jax.ShapeDtypeStruct((depth, b, d, m), BF16),
            jax.ShapeDtypeStruct((depth, b, m, d), BF16),
        ],
        compiler_params=_params("arbitrary", "arbitrary"),
        name="memory_kv",
    )(mem, _row(mem_norm_g), wkv.astype(BF16))


def _pool_tile(x, gpre_ref, w_ref, b_ref, scale_ref, gpost_ref, ext_ref, tiles_per_seq):
    tm, d = x.shape
    c = d // len(POOL_WINDOWS)
    t = pl.program_id(0) % tiles_per_seq

    @pl.when(t == 0)
    def _():
        ext_ref[0:POOL_HALO, :] = jnp.zeros((POOL_HALO, d), F32)

    hn = _rms(x, gpre_ref[...])
    ext_ref[POOL_HALO:, :] = hn

    pos = (t * tm + lax.broadcasted_iota(jnp.int32, (tm, 1), 0) + 1).astype(F32)
    ys = []
    for g, w in enumerate(POOL_WINDOWS):
        s = ext_ref[:, g * c:(g + 1) * c]
        span = 1
        while span < w:
            s = s + pltpu.roll(s, span, 0)
            span *= 2
        pooled = s[POOL_HALO:, :] / jnp.minimum(pos, float(w)) - hn[:, g * c:(g + 1) * c]
        y = jnp.dot(pooled.astype(BF16), w_ref[g], preferred_element_type=F32)
        ys.append(y + b_ref[:, g * c:(g + 1) * c])
    y = jnp.concatenate(ys, axis=-1) * scale_ref[...]
    ext_ref[0:POOL_HALO, :] = hn[tm - POOL_HALO:, :]
    return x + _rms(y, gpost_ref[...])


def _xattn_tile(x, kt_ref, v_ref, wq_ref, wo_ref, gpre_ref, gpost_ref):
    d = x.shape[-1]
    hd = d // N_XA_HEADS
    h = _rms(x, gpre_ref[...]).astype(BF16)
    q = jnp.dot(h, wq_ref[...], preferred_element_type=F32) * (hd ** -0.5)
    heads = []
    for i in range(N_XA_HEADS):
        sl = slice(i * hd, (i + 1) * hd)
        s = jnp.dot(q[:, sl].astype(BF16), kt_ref[0, sl, :], preferred_element_type=F32)
        p = jnp.exp(s - jnp.max(s, axis=-1, keepdims=True))
        p = p / jnp.sum(p, axis=-1, keepdims=True)
        heads.append(jnp.dot(p.astype(BF16), v_ref[0, :, sl], preferred_element_type=F32).astype(BF16))
    y = jnp.dot(jnp.concatenate(heads, axis=-1), wo_ref[...], preferred_element_type=F32)
    return x + _rms(y, gpost_ref[...])


def _route_tile(x, gffn_ref, wr_ref, meta_ref, metat_ref, cnt_ref, upper_ref, n_experts):
    tm = x.shape[0]

    @pl.when(pl.program_id(0) == 0)
    def _():
        before = lax.broadcasted_iota(jnp.int32, (tm, tm), 0) < lax.broadcasted_iota(jnp.int32, (tm, tm), 1)
        upper_ref[...] = jnp.where(before, 1.0, 0.0).astype(BF16)

    hn = _rms(x, gffn_ref[...]).astype(BF16)
    logits = jnp.dot(hn, wr_ref[...], preferred_element_type=F32)
    lg = logits.T[0:SUBLANES, :]
    sub = lax.broadcasted_iota(jnp.int32, (SUBLANES, tm), 0).astype(F32)
    lg = jnp.where(sub < n_experts, lg, -jnp.inf)
    m1 = jnp.max(lg, axis=0, keepdims=True)
    e1 = jnp.min(jnp.where(lg == m1, sub, float(SUBLANES)), axis=0, keepdims=True)
    rest = jnp.where(sub == e1, -jnp.inf, lg)
    m2 = jnp.max(rest, axis=0, keepdims=True)
    e2 = jnp.min(jnp.where(rest == m2, sub, float(SUBLANES)), axis=0, keepdims=True)
    ex = jnp.exp(m2 - m1)
    w1 = 1.0 / (1.0 + ex)
    w2 = ex / (1.0 + ex)

    hit1 = sub == e1
    hit2 = sub == e2
    onehot = jnp.where(hit1 | hit2, 1.0, 0.0)
    earlier = jnp.dot(onehot.astype(BF16), upper_ref[...], preferred_element_type=F32)
    counts = jnp.sum(onehot, axis=1, keepdims=True)
    cnt_ref[...] = jnp.broadcast_to(counts, cnt_ref.shape)
    chunk = jnp.ceil(counts * (1.0 / ROW_ALIGN)) * ROW_ALIGN
    lower = (lax.broadcasted_iota(jnp.int32, (SUBLANES, SUBLANES), 1)
             < lax.broadcasted_iota(jnp.int32, (SUBLANES, SUBLANES), 0))
    chunk_start = jnp.dot(jnp.where(lower, 1.0, 0.0).astype(BF16), jnp.broadcast_to(chunk, (SUBLANES, LANES)).astype(BF16),
                          preferred_element_type=F32)[:, 0:1]
    place = chunk_start + earlier
    r1 = jnp.sum(jnp.where(hit1, place, 0.0), axis=0, keepdims=True)
    r2 = jnp.sum(jnp.where(hit2, place, 0.0), axis=0, keepdims=True)

    rows = jnp.zeros((SUBLANES, tm), F32)
    for k, val in enumerate((e1, e2, r1, r2, w1, w2)):
        rows = jnp.where(sub == float(k), val, rows)
    metat_ref[...] = rows
    cols = jnp.concatenate([rows, jnp.zeros((LANES - SUBLANES, tm), F32)], axis=0).T
    meta_ref[...] = cols[:, :meta_ref.shape[1]]


def _xattn_kernel(*refs, tiles_per_seq, pool, n_experts):
    it = iter(refs)

    def take(k):
        return [next(it) for _ in range(k)]

    (x_ref,) = take(1)
    pool_in = take(5) if pool else None
    attn_in = take(6)
    route_in = take(2) if n_experts else None
    (o_ref,) = take(1)
    route_out = take(3) if n_experts else None
    (ext_ref,) = take(1) if pool else (None,)
    (upper_ref,) = take(1) if n_experts else (None,)

    x = x_ref[...]
    if pool:
        x = _pool_tile(x, *pool_in, ext_ref, tiles_per_seq)
    xn = _xattn_tile(x, *attn_in)
    if n_experts:
        _route_tile(xn, *route_in, *route_out, upper_ref, n_experts)
    o_ref[...] = xn


def _xattn_layer(x, *, kt, v, wq, wo, gpre, gpost, seq, tm, pool=None, route=None):
    n, d = x.shape
    m = kt.shape[-1]
    tps = seq // tm
    tok = pl.BlockSpec((tm, d), lambda i: (i, 0))
    vec = _full((1, d))
    in_specs, args, scratch = [tok], [x], []
    if pool is not None:
        pg, w_grp, b_grp, scale, ppost = pool
        in_specs += [vec, _full(w_grp.shape), vec, vec, vec]
        args += [_row(pg), w_grp.astype(BF16), _row(b_grp), _row(scale), _row(ppost)]
        scratch.append(pltpu.VMEM((tm + POOL_HALO, d), F32))
    in_specs += [pl.BlockSpec((1, d, m), lambda i: (i // tps, 0, 0)), pl.BlockSpec((1, m, d), lambda i: (i // tps, 0, 0)),
                 _full((d, d)), _full((d, d)), vec, vec]
    args += [kt, v, wq.astype(BF16), wo.astype(BF16), _row(gpre), _row(gpost)]
    out_specs, out_shape, n_experts = [tok], [jax.ShapeDtypeStruct((n, d), F32)], 0
    if route is not None:
        gffn, w_router = route
        n_experts = w_router.shape[1]
        assert n_experts <= SUBLANES
        in_specs += [vec, _full((d, LANES))]
        args += [_row(gffn), jnp.pad(w_router, ((0, 0), (0, LANES - n_experts))).astype(BF16)]
        out_specs += [pl.BlockSpec((tm, SUBLANES), lambda i: (i, 0)), pl.BlockSpec((SUBLANES, tm), lambda i: (i, 0)),
                      pl.BlockSpec((SUBLANES, LANES), lambda i: (i, 0))]
        out_shape += [jax.ShapeDtypeStruct((n, SUBLANES), F32), jax.ShapeDtypeStruct((n // tm * SUBLANES, tm), F32),
                      jax.ShapeDtypeStruct((n // tm * SUBLANES, LANES), F32)]
        scratch.append(pltpu.VMEM((tm, tm), BF16))
    out = pl.pallas_call(
        functools.partial(_xattn_kernel, tiles_per_seq=tps, pool=pool is not None, n_experts=n_experts),
        grid=(n // tm,),
        in_specs=in_specs,
        out_specs=out_specs,
        out_shape=out_shape,
        scratch_shapes=scratch,
        compiler_params=_params("arbitrary"),
        name="xattn" + ("_pool" if pool is not None else "") + ("_route" if route is not None else ""),
    )(*args)
    return out if route is not None else out[0]


def _swiglu_step(x_ref, gpre_ref, wg_ref, wu_ref, wd_ref, hn_ref, acc_ref):
    j = pl.program_id(1)

    @pl.when(j == 0)
    def _():
        x = x_ref[...]
        hn_ref[...] = (x if gpre_ref is None else _rms(x, gpre_ref[...])).astype(BF16)
        acc_ref[...] = jnp.zeros_like(acc_ref)

    hn = hn_ref[...]
    gate = jnp.dot(hn, wg_ref[...].astype(BF16), preferred_element_type=F32)
    up = jnp.dot(hn, wu_ref[...].astype(BF16), preferred_element_type=F32)
    act = (gate * jax.nn.sigmoid(gate) * up).astype(BF16)
    acc_ref[...] += jnp.dot(act, wd_ref[...].astype(BF16), preferred_element_type=F32)


def _ffn_kernel(x_ref, gpre_ref, wg_ref, wu_ref, wd_ref, gpost_ref, o_ref, hn_ref, acc_ref):
    _swiglu_step(x_ref, gpre_ref, wg_ref, wu_ref, wd_ref, hn_ref, acc_ref)

    @pl.when(pl.program_id(1) == pl.num_programs(1) - 1)
    def _():
        o_ref[...] = x_ref[...] + _rms(acc_ref[...], gpost_ref[...])


def _ffn_layer(x, gpre, w_gu, w_down, gpost, *, tm, fb):
    n, d = x.shape
    f = w_down.shape[0]
    nf = f // fb
    tok = pl.BlockSpec((tm, d), lambda i, j: (i, 0))
    return pl.pallas_call(
        _ffn_kernel,
        grid=(n // tm, nf),
        in_specs=[
            tok, _full((1, d)),
            pl.BlockSpec((d, fb), lambda i, j: (0, j)),
            pl.BlockSpec((d, fb), lambda i, j: (0, nf + j)),
            pl.BlockSpec((fb, d), lambda i, j: (j, 0)),
            _full((1, d)),
        ],
        out_specs=tok,
        out_shape=jax.ShapeDtypeStruct((n, d), F32),
        scratch_shapes=[pltpu.VMEM((tm, d), BF16), pltpu.VMEM((tm, d), F32)],
        compiler_params=_params("arbitrary", "arbitrary"),
        name="dense_swiglu",
    )(x, _row(gpre), w_gu, w_gu, w_down, _row(gpost))


def _experts_kernel(tile_expert_ref, ntiles_ref, x_ref, wg_ref, wu_ref, wd_ref, y_ref, hn_ref):
    i = pl.program_id(0)
    last = pl.program_id(1) == pl.num_programs(1) - 1
    live = i < ntiles_ref[0]

    @pl.when(live)
    def _():
        _swiglu_step(x_ref, None, wg_ref.at[0], wu_ref.at[0], wd_ref.at[0], hn_ref, y_ref)

    @pl.when(jnp.logical_not(live) & last)
    def _():
        y_ref[...] = jnp.zeros_like(y_ref)


def _experts_layer(xs, w_gu, w_down, tile_expert, ntiles, *, te, fb):
    r, d = xs.shape
    _, f, _ = w_down.shape
    nf = f // fb

    def x_map(i, j, te_ref, nt_ref):
        return (jnp.minimum(i, nt_ref[0] - 1), 0)

    def fblock(i, j, nt_ref):
        return jnp.where(i < nt_ref[0], j, nf - 1)

    def expert(i, te_ref, nt_ref):
        return te_ref[jnp.minimum(i, nt_ref[0] - 1)]

    grid_spec = pltpu.PrefetchScalarGridSpec(
        num_scalar_prefetch=2,
        grid=(r // te, nf),
        in_specs=[
            pl.BlockSpec((te, d), x_map),
            pl.BlockSpec((1, d, fb), lambda i, j, te_ref, nt_ref: (expert(i, te_ref, nt_ref), 0, fblock(i, j, nt_ref))),
            pl.BlockSpec((1, d, fb),
                         lambda i, j, te_ref, nt_ref: (expert(i, te_ref, nt_ref), 0, nf + fblock(i, j, nt_ref))),
            pl.BlockSpec((1, fb, d), lambda i, j, te_ref, nt_ref: (expert(i, te_ref, nt_ref), fblock(i, j, nt_ref), 0)),
        ],
        out_specs=pl.BlockSpec((te, d), lambda i, j, te_ref, nt_ref: (i, 0)),
        scratch_shapes=[pltpu.VMEM((te, d), BF16)],
    )
    return pl.pallas_call(
        _experts_kernel,
        grid_spec=grid_spec,
        out_shape=jax.ShapeDtypeStruct((r, d), F32),
        compiler_params=_params("arbitrary", "arbitrary"),
        name="moe_experts",
    )(tile_expert, ntiles, xs, w_gu, w_gu, w_down)


def _conv_kernel(x_ref, gpre_ref, w1_ref, b1_ref, dw_ref, dwb_ref, lng_ref, lnb_ref, w2_ref, b2_ref, gpost_ref,
                 o_ref, uext_ref, v_ref, *, tiles_per_seq, width):
    tm, d = x_ref.shape
    nblk = d // LANES
    group = 2
    t = pl.program_id(0) % tiles_per_seq

    @pl.when(t == 0)
    def _():
        uext_ref[:, 0:CONV_HALO, :] = jnp.zeros((nblk, CONV_HALO, LANES), F32)

    x = x_ref[...]
    h = _rms(x, gpre_ref[...]).astype(BF16)
    first = CONV_HALO - (width - 1)
    for g0 in range(0, nblk, group):
        lo, hi = g0 * LANES, (g0 + group) * LANES
        val = jnp.dot(h, w1_ref[:, lo:hi], preferred_element_type=F32) + b1_ref[:, lo:hi]
        gate = jnp.dot(h, w1_ref[:, d + lo:d + hi], preferred_element_type=F32) + b1_ref[:, d + lo:d + hi]
        u = val * jax.nn.sigmoid(gate)
        for c in range(g0, g0 + group):
            uext_ref[c, CONV_HALO:, :] = u[:, (c - g0) * LANES:(c - g0 + 1) * LANES]
            taps = [dw_ref[c, k:k + 1, :] for k in range(width)]
            for r0 in range(0, tm, CONV_ROWS):
                acc = taps[0] * uext_ref[c, r0 + first:r0 + first + CONV_ROWS, :]
                for k in range(1, width):
                    acc = acc + taps[k] * uext_ref[c, r0 + first + k:r0 + first + k + CONV_ROWS, :]
                v_ref[c, r0:r0 + CONV_ROWS, :] = acc
            uext_ref[c, 0:CONV_HALO, :] = uext_ref[c, tm:tm + CONV_HALO, :]

    v = jnp.concatenate([v_ref[c] for c in range(nblk)], axis=-1) + dwb_ref[...]
    mu = jnp.mean(v, axis=-1, keepdims=True)
    vc = v - mu
    var = jnp.mean(vc * vc, axis=-1, keepdims=True)
    z = vc * lax.rsqrt(var + EPS) * lng_ref[...] + lnb_ref[...]
    z = (z * jax.nn.sigmoid(z)).astype(BF16)
    y = jnp.dot(z, w2_ref[...], preferred_element_type=F32) + b2_ref[...]
    o_ref[...] = x + _rms(y, gpost_ref[...])


def _conv_layer(x, gpre, pw1_w, pw1_b, dw_w, dw_b, ln_g, ln_b, pw2_w, pw2_b, gpost, *, seq, tm):
    n, d = x.shape
    width = dw_w.shape[0]
    nblk = d // LANES
    kpad = -(-width // SUBLANES) * SUBLANES
    dw = jnp.pad(dw_w, ((0, kpad - width), (0, 0))).reshape(kpad, nblk, LANES).transpose(1, 0, 2)
    tok = pl.BlockSpec((tm, d), lambda i: (i, 0))
    vec = _full((1, d))
    return pl.pallas_call(
        functools.partial(_conv_kernel, tiles_per_seq=seq // tm, width=width),
        grid=(n // tm,),
        in_specs=[tok, vec, _full((d, 2 * d)), _full((1, 2 * d)), _full((nblk, kpad, LANES)),
                  vec, vec, vec, _full((d, d)), vec, vec],
        out_specs=tok,
        out_shape=jax.ShapeDtypeStruct((n, d), F32),
        scratch_shapes=[pltpu.VMEM((nblk, tm + CONV_HALO, LANES), F32), pltpu.VMEM((nblk, tm, LANES), F32)],
        compiler_params=_params("arbitrary"),
        name="conv_module",
    )(x, _row(gpre), pw1_w.astype(BF16), _row(pw1_b), dw, _row(dw_b), _row(ln_g), _row(ln_b),
      pw2_w.astype(BF16), _row(pw2_b), _row(gpost))


def _chunk_pieces(tbl_ref, tile, n_experts, max_piece, visit):
    for e in range(n_experts):
        base, length, off = tbl_ref[tile, e], tbl_ref[tile, n_experts + e], tbl_ref[tile, 2 * n_experts + e]
        p = max_piece
        while p >= ROW_ALIGN:
            done = jnp.bitwise_and(length, ~(2 * p - 1))

            @pl.when(jnp.bitwise_and(length, p) != 0)
            def _(p=p, done=done):
                visit(p, pl.multiple_of(off + done, ROW_ALIGN), pl.multiple_of(base + done, ROW_ALIGN))

            p //= 2


def _dispatch_kernel(tbl_ref, fill_start_ref, fill_len_ref, x_ref, gpre_ref, metat_ref, xs_ref, xp_ref, zero_ref,
                     sem, zsem, *, n_experts):
    tm, d = x_ref.shape
    rows = xp_ref.shape[1]
    i = pl.program_id(0)
    last = pl.num_programs(0) - 1
    slot = i % 2

    hn = _rms(x_ref[...], gpre_ref[...]).astype(BF16)
    row_id = lax.broadcasted_iota(jnp.int32, (rows, tm), 0)
    place = metat_ref[TOP_K:2 * TOP_K, :].astype(jnp.int32)
    hit = (row_id == place[0:1, :]) | (row_id == place[1:2, :])
    perm = jnp.where(hit, 1.0, 0.0).astype(BF16)
    xp_ref[slot] = jnp.dot(perm, hn, preferred_element_type=F32)

    def chunk_copy(buf, p, tile_row, global_row):
        return pltpu.make_async_copy(xp_ref.at[buf, pl.ds(tile_row, p), :], xs_ref.at[pl.ds(global_row, p), :],
                                     sem.at[buf])

    _chunk_pieces(tbl_ref, i, n_experts, tm, lambda *a: chunk_copy(slot, *a).start())

    @pl.when(i > 0)
    def _():
        _chunk_pieces(tbl_ref, i - 1, n_experts, tm, lambda *a: chunk_copy(1 - slot, *a).wait())

    @pl.when(i == last)
    def _():
        zero_ref[...] = jnp.zeros_like(zero_ref)
        zrows = zero_ref.shape[0]
        for e in range(n_experts + 1):
            step = ROW_ALIGN if e < n_experts else zrows

            def zero_copy(r, e=e, step=step):
                row0 = pl.multiple_of(fill_start_ref[e] + r * step, step)
                return pltpu.make_async_copy(zero_ref.at[pl.ds(0, step), :], xs_ref.at[pl.ds(row0, step), :], zsem)

            lax.fori_loop(0, fill_len_ref[e], lambda r, c: (zero_copy(r).start(), c)[1], 0)
            lax.fori_loop(0, fill_len_ref[e], lambda r, c: (zero_copy(r).wait(), c)[1], 0)
        _chunk_pieces(tbl_ref, i, n_experts, tm, lambda *a: chunk_copy(slot, *a).wait())


def _dispatch(x, gpre, tbl, metat, fill_start, fill_len, *, rows, tm, tile_rows, zrows, n_experts):
    n, d = x.shape
    smem = pl.BlockSpec(memory_space=pltpu.SMEM)
    return pl.pallas_call(
        functools.partial(_dispatch_kernel, n_experts=n_experts),
        grid=(n // tm,),
        in_specs=[smem, smem, smem, pl.BlockSpec((tm, d), lambda i: (i, 0)), _full((1, d)),
                  pl.BlockSpec((SUBLANES, tm), lambda i: (i, 0))],
        out_specs=pl.BlockSpec(memory_space=pl.ANY),
        out_shape=jax.ShapeDtypeStruct((rows, d), F32),
        scratch_shapes=[pltpu.VMEM((2, tile_rows, d), F32), pltpu.VMEM((zrows, d), F32),
                        pltpu.SemaphoreType.DMA((2,)), pltpu.SemaphoreType.DMA],
        compiler_params=_params("arbitrary"),
        name="moe_dispatch",
    )(tbl, fill_start, fill_len, x, _row(gpre), metat)


def _combine_kernel(tbl_ref, x_ref, meta_ref, ys_ref, gpost_ref, o_ref, yp_ref, sem, *, n_experts):
    tm, d = x_ref.shape
    rows = yp_ref.shape[1]
    i = pl.program_id(0)
    slot = i % 2

    def chunk_copy(buf, p, tile_row, global_row):
        return pltpu.make_async_copy(ys_ref.at[pl.ds(global_row, p), :], yp_ref.at[buf, pl.ds(tile_row, p), :],
                                     sem.at[buf])

    @pl.when(i == 0)
    def _():
        yp_ref[...] = jnp.zeros_like(yp_ref)
        _chunk_pieces(tbl_ref, i, n_experts, tm, lambda *a: chunk_copy(slot, *a).start())

    @pl.when(i + 1 < pl.num_programs(0))
    def _():
        _chunk_pieces(tbl_ref, i + 1, n_experts, tm, lambda *a: chunk_copy(1 - slot, *a).start())

    _chunk_pieces(tbl_ref, i, n_experts, tm, lambda *a: chunk_copy(slot, *a).wait())

    yp = yp_ref[slot].astype(BF16)
    row_id = lax.broadcasted_iota(jnp.int32, (tm, rows), 1)
    meta = meta_ref[...]
    place = meta[:, TOP_K:2 * TOP_K].astype(jnp.int32)
    y = None
    for k in range(TOP_K):
        pick = jnp.where(row_id == place[:, k:k + 1], 1.0, 0.0).astype(BF16)
        yk = jnp.dot(pick, yp, preferred_element_type=F32) * meta[:, 2 * TOP_K + k:2 * TOP_K + k + 1]
        y = yk if y is None else y + yk
    o_ref[...] = x_ref[...] + _rms(y, gpost_ref[...])


def _combine(x, tbl, meta, ys, gpost, *, tm, tile_rows, n_experts):
    n, d = x.shape
    tok = pl.BlockSpec((tm, d), lambda i: (i, 0))
    return pl.pallas_call(
        functools.partial(_combine_kernel, n_experts=n_experts),
        grid=(n // tm,),
        in_specs=[
            pl.BlockSpec(memory_space=pltpu.SMEM),
            tok,
            pl.BlockSpec((tm, SUBLANES), lambda i: (i, 0)),
            pl.BlockSpec(memory_space=pl.ANY),
            _full((1, d)),
        ],
        out_specs=tok,
        out_shape=jax.ShapeDtypeStruct((n, d), F32),
        scratch_shapes=[pltpu.VMEM((2, tile_rows, d), F32), pltpu.SemaphoreType.DMA((2,))],
        compiler_params=_params("arbitrary"),
        name="moe_combine",
    )(tbl, x, meta, ys, _row(gpost))


def _moe_layer(x, meta, metat, cnt, gpre, w_gu, w_down, gpost, *, tm, te, fb):
    n, d = x.shape
    e = w_gu.shape[0]
    nt = n // tm

    counts = cnt.reshape(nt, SUBLANES, LANES)[:, :e, 0].astype(jnp.int32)
    chunk = (counts + ROW_ALIGN - 1) // ROW_ALIGN * ROW_ALIGN
    tile_off = jnp.cumsum(chunk, axis=1) - chunk
    total = jnp.sum(chunk, axis=0)
    padded = (total + te - 1) // te * te
    ends = jnp.cumsum(padded)
    starts = ends - padded
    base = starts[None, :] + jnp.cumsum(chunk, axis=0) - chunk
    tbl = jnp.concatenate([base, chunk, tile_off], axis=1)
    tile_rows = -(-(TOP_K * tm + e * ROW_ALIGN) // LANES) * LANES

    rows = TOP_K * n + e * ROW_ALIGN * nt + e * te
    rows = -(-rows // te) * te
    ntiles = (ends[-1] // te).reshape(1)
    tile_expert = jnp.minimum(jnp.searchsorted(ends // te, jnp.arange(rows // te, dtype=jnp.int32), side="right"),
                              e - 1).astype(jnp.int32)

    zrows = min(te, ZERO_ROWS)
    assert te % zrows == 0 and zrows % ROW_ALIGN == 0
    fill_start = jnp.concatenate([starts + total, ends[-1:]])
    fill_len = jnp.concatenate([(padded - total) // ROW_ALIGN, (rows - ends[-1:]) // zrows])
    xs = _dispatch(x, gpre, tbl, metat, fill_start, fill_len, rows=rows, tm=tm, tile_rows=tile_rows, zrows=zrows,
                   n_experts=e)
    ys = _experts_layer(xs, w_gu, w_down, tile_expert, ntiles, te=te, fb=fb)
    return _combine(x, tbl, meta, ys, gpost, tm=tm, tile_rows=tile_rows, n_experts=e)


def _tile(n, want):
    t = min(n, want)
    while n % t:
        t //= 2
    return t


@jax.jit
def kernel(x, mem, mem_norm_g, mix_pre_g, mix_post_g, xa_pre_g, xa_post_g, ffn_pre_g, ffn_post_g, xa_wq, xa_wkv, xa_wo, pool_w, pool_b, pool_scale, conv_pw1_w, conv_pw1_b, conv_dw_w, conv_dw_b, conv_ln_g, conv_ln_b, conv_pw2_w, conv_pw2_b, ffn_w_gu, ffn_w_down, moe_router, moe_w_gu, moe_w_down):
    b, s, d = x.shape
    depth = mix_pre_g.shape[0]
    f = ffn_w_down.shape[1]
    tm = _tile(s, 512)
    tf = _tile(s, 1024)
    fb = _tile(f, 512)
    kt, v = _memory_kv(mem, mem_norm_g, xa_wkv)
    h = x.reshape(b * s, d)
    for i in range(depth):
        j = i // 2
        xattn = functools.partial(_xattn_layer, kt=kt[i], v=v[i], wq=xa_wq[i], wo=xa_wo[i], gpre=xa_pre_g[i],
                                  gpost=xa_post_g[i], seq=s)
        if i % 2 == 0:
            h = xattn(h, pool=(mix_pre_g[i], pool_w[j], pool_b[j], pool_scale[j], mix_post_g[i]), tm=tf)
            h = _ffn_layer(h, ffn_pre_g[i], ffn_w_gu[j], ffn_w_down[j], ffn_post_g[i], tm=tf, fb=fb)
        else:
            h = _conv_layer(h, mix_pre_g[i], conv_pw1_w[j], conv_pw1_b[j], conv_dw_w[j], conv_dw_b[j], conv_ln_g[j],
                            conv_ln_b[j], conv_pw2_w[j], conv_pw2_b[j], mix_post_g[i], seq=s, tm=tm)
            h, meta, metat, cnt = xattn(h, route=(ffn_pre_g[i], moe_router[j]), tm=tm)
            h = _moe_layer(h, meta, metat, cnt, ffn_pre_g[i], moe_w_gu[j], moe_w_down[j], ffn_post_g[i], tm=tm, te=tf, fb=fb)
    return h.reshape(b, s, d)
```

```python
import functools

import jax
import jax.numpy as jnp
from jax import lax
from jax.experimental import pallas as pl
from jax.experimental.pallas import tpu as pltpu

EPS = 1e-6
N_XA_HEADS = 4
POOL_WINDOWS = (2, 4, 8, 16)
TOP_K = 2

LANES = 128
SUBLANES = 8
POOL_HALO = 16
CONV_HALO = 32
CONV_ROWS = 64
ROW_ALIGN = SUBLANES
ZERO_ROWS = 256
VMEM_LIMIT = 56 * 1024 * 1024

F32 = jnp.float32
BF16 = jnp.bfloat16


def _rms(x, g):
    return x * lax.rsqrt(jnp.mean(x * x, axis=-1, keepdims=True) + EPS) * g


def _params(*sem):
    return pltpu.CompilerParams(dimension_semantics=sem, vmem_limit_bytes=VMEM_LIMIT)


def _row(v):
    return v.reshape(1, -1).astype(F32)


def _full(shape):
    nd = len(shape)
    return pl.BlockSpec(shape, lambda *_: (0,) * nd)


def _kv_kernel(mem_ref, g_ref, wkv_ref, kt_ref, v_ref):
    d = mem_ref.shape[-1]
    mn = _rms(mem_ref[0], g_ref[...]).astype(BF16)
    kv = jnp.dot(mn, wkv_ref[0], preferred_element_type=F32)
    kt_ref[0, 0] = kv[:, :d].T.astype(BF16)
    v_ref[0, 0] = kv[:, d:].astype(BF16)


def _memory_kv(mem, mem_norm_g, wkv):
    depth, d, _ = wkv.shape
    b, m, _ = mem.shape
    return pl.pallas_call(
        _kv_kernel,
        grid=(depth, b),
        in_specs=[
            pl.BlockSpec((1, m, d), lambda l, i: (i, 0, 0)),
            _full((1, d)),
            pl.BlockSpec((1, d, 2 * d), lambda l, i: (l, 0, 0)),
        ],
        out_specs=[
            pl.BlockSpec((1, 1, d, m), lambda l, i: (l, i, 0, 0)),
            pl.BlockSpec((1, 1, m, d), lambda l, i: (l, i, 0, 0)),
        ],
        out_shape=[
            jax.ShapeDtypeStruct((depth, b, d, m), BF16),
            jax.ShapeDtypeStruct((depth, b, m, d), BF16),
        ],
        compiler_params=_params("arbitrary", "arbitrary"),
        name="memory_kv",
    )(mem, _row(mem_norm_g), wkv.astype(BF16))


def _pool_tile(x, gpre_ref, w_ref, b_ref, scale_ref, gpost_ref, ext_ref, tiles_per_seq):
    tm, d = x.shape
    c = d // len(POOL_WINDOWS)
    t = pl.program_id(0) % tiles_per_seq

    @pl.when(t == 0)
    def _():
        ext_ref[0:POOL_HALO, :] = jnp.zeros((POOL_HALO, d), F32)

    hn = _rms(x, gpre_ref[...])
    ext_ref[POOL_HALO:, :] = hn

    pos = (t * tm + lax.broadcasted_iota(jnp.int32, (tm, 1), 0) + 1).astype(F32)
    ys = []
    for g, w in enumerate(POOL_WINDOWS):
        s = ext_ref[:, g * c:(g + 1) * c]
        span = 1
        while span < w:
            s = s + pltpu.roll(s, span, 0)
            span *= 2
        pooled = s[POOL_HALO:, :] / jnp.minimum(pos, float(w)) - hn[:, g * c:(g + 1) * c]
        y = jnp.dot(pooled.astype(BF16), w_ref[g], preferred_element_type=F32)
        ys.append(y + b_ref[:, g * c:(g + 1) * c])
    y = jnp.concatenate(ys, axis=-1) * scale_ref[...]
    ext_ref[0:POOL_HALO, :] = hn[tm - POOL_HALO:, :]
    return x + _rms(y, gpost_ref[...])


def _xattn_tile(x, kt_ref, v_ref, wq_ref, wo_ref, gpre_ref, gpost_ref):
    d = x.shape[-1]
    hd = d // N_XA_HEADS
    h = _rms(x, gpre_ref[...]).astype(BF16)
    q = jnp.dot(h, wq_ref[...], preferred_element_type=F32) * (hd ** -0.5)
    heads = []
    for i in range(N_XA_HEADS):
        sl = slice(i * hd, (i + 1) * hd)
        s = jnp.dot(q[:, sl].astype(BF16), kt_ref[0, sl, :], preferred_element_type=F32)
        p = jnp.exp(s - jnp.max(s, axis=-1, keepdims=True))
        p = p / jnp.sum(p, axis=-1, keepdims=True)
        heads.append(jnp.dot(p.astype(BF16), v_ref[0, :, sl], preferred_element_type=F32).astype(BF16))
    y = jnp.dot(jnp.concatenate(heads, axis=-1), wo_ref[...], preferred_element_type=F32)
    return x + _rms(y, gpost_ref[...])


def _route_tile(x, gffn_ref, wr_ref, meta_ref, metat_ref, cnt_ref, upper_ref, n_experts):
    tm = x.shape[0]

    @pl.when(pl.program_id(0) == 0)
    def _():
        before = lax.broadcasted_iota(jnp.int32, (tm, tm), 0) < lax.broadcasted_iota(jnp.int32, (tm, tm), 1)
        upper_ref[...] = jnp.where(before, 1.0, 0.0).astype(BF16)

    hn = _rms(x, gffn_ref[...]).astype(BF16)
    logits = jnp.dot(hn, wr_ref[...], preferred_element_type=F32)
    lg = logits.T[0:SUBLANES, :]
    sub = lax.broadcasted_iota(jnp.int32, (SUBLANES, tm), 0).astype(F32)
    lg = jnp.where(sub < n_experts, lg, -jnp.inf)
    m1 = jnp.max(lg, axis=0, keepdims=True)
    e1 = jnp.min(jnp.where(lg == m1, sub, float(SUBLANES)), axis=0, keepdims=True)
    rest = jnp.where(sub == e1, -jnp.inf, lg)
    m2 = jnp.max(rest, axis=0, keepdims=True)
    e2 = jnp.min(jnp.where(rest == m2, sub, float(SUBLANES)), axis=0, keepdims=True)
    ex = jnp.exp(m2 - m1)
    w1 = 1.0 / (1.0 + ex)
    w2 = ex / (1.0 + ex)

    hit1 = sub == e1
    hit2 = sub == e2
    onehot = jnp.where(hit1 | hit2, 1.0, 0.0)
    earlier = jnp.dot(onehot.astype(BF16), upper_ref[...], preferred_element_type=F32)
    counts = jnp.sum(onehot, axis=1, keepdims=True)
    cnt_ref[...] = jnp.broadcast_to(counts, cnt_ref.shape)
    chunk = jnp.ceil(counts * (1.0 / ROW_ALIGN)) * ROW_ALIGN
    lower = (lax.broadcasted_iota(jnp.int32, (SUBLANES, SUBLANES), 1)
             < lax.broadcasted_iota(jnp.int32, (SUBLANES, SUBLANES), 0))
    chunk_start = jnp.dot(jnp.where(lower, 1.0, 0.0).astype(BF16), jnp.broadcast_to(chunk, (SUBLANES, LANES)).astype(BF16),
                          preferred_element_type=F32)[:, 0:1]
    place = chunk_start + earlier
    r1 = jnp.sum(jnp.where(hit1, place, 0.0), axis=0, keepdims=True)
    r2 = jnp.sum(jnp.where(hit2, place, 0.0), axis=0, keepdims=True)

    rows = jnp.zeros((SUBLANES, tm), F32)
    for k, val in enumerate((e1, e2, r1, r2, w1, w2)):
        rows = jnp.where(sub == float(k), val, rows)
    metat_ref[...] = rows
    cols = jnp.concatenate([rows, jnp.zeros((LANES - SUBLANES, tm), F32)], axis=0).T
    meta_ref[...] = cols[:, :meta_ref.shape[1]]


def _xattn_kernel(*refs, tiles_per_seq, pool, n_experts):
    it = iter(refs)

    def take(k):
        return [next(it) for _ in range(k)]

    (x_ref,) = take(1)
    pool_in = take(5) if pool else None
    attn_in = take(6)
    route_in = take(2) if n_experts else None
    (o_ref,) = take(1)
    route_out = take(3) if n_experts else None
    (ext_ref,) = take(1) if pool else (None,)
    (upper_ref,) = take(1) if n_experts else (None,)

    x = x_ref[...]
    if pool:
        x = _pool_tile(x, *pool_in, ext_ref, tiles_per_seq)
    xn = _xattn_tile(x, *attn_in)
    if n_experts:
        _route_tile(xn, *route_in, *route_out, upper_ref, n_experts)
    o_ref[...] = xn


def _xattn_layer(x, *, kt, v, wq, wo, gpre, gpost, seq, tm, pool=None, route=None):
    n, d = x.shape
    m = kt.shape[-1]
    tps = seq // tm
    tok = pl.BlockSpec((tm, d), lambda i: (i, 0))
    vec = _full((1, d))
    in_specs, args, scratch = [tok], [x], []
    if pool is not None:
        pg, w_grp, b_grp, scale, ppost = pool
        in_specs += [vec, _full(w_grp.shape), vec, vec, vec]
        args += [_row(pg), w_grp.astype(BF16), _row(b_grp), _row(scale), _row(ppost)]
        scratch.append(pltpu.VMEM((tm + POOL_HALO, d), F32))
    in_specs += [pl.BlockSpec((1, d, m), lambda i: (i // tps, 0, 0)), pl.BlockSpec((1, m, d), lambda i: (i // tps, 0, 0)),
                 _full((d, d)), _full((d, d)), vec, vec]
    args += [kt, v, wq.astype(BF16), wo.astype(BF16), _row(gpre), _row(gpost)]
    out_specs, out_shape, n_experts = [tok], [jax.ShapeDtypeStruct((n, d), F32)], 0
    if route is not None:
        gffn, w_router = route
        n_experts = w_router.shape[1]
        assert n_experts <= SUBLANES
        in_specs += [vec, _full((d, LANES))]
        args += [_row(gffn), jnp.pad(w_router, ((0, 0), (0, LANES - n_experts))).astype(BF16)]
        out_specs += [pl.BlockSpec((tm, SUBLANES), lambda i: (i, 0)), pl.BlockSpec((SUBLANES, tm), lambda i: (i, 0)),
                      pl.BlockSpec((SUBLANES, LANES), lambda i: (i, 0))]
        out_shape += [jax.ShapeDtypeStruct((n, SUBLANES), F32), jax.ShapeDtypeStruct((n // tm * SUBLANES, tm), F32),
                      jax.ShapeDtypeStruct((n // tm * SUBLANES, LANES), F32)]
        scratch.append(pltpu.VMEM((tm, tm), BF16))
    out = pl.pallas_call(
        functools.partial(_xattn_kernel, tiles_per_seq=tps, pool=pool is not None, n_experts=n_experts),
        grid=(n // tm,),
        in_specs=in_specs,
        out_specs=out_specs,
        out_shape=out_shape,
        scratch_shapes=scratch,
        compiler_params=_params("arbitrary"),
        name="xattn" + ("_pool" if pool is not None else "") + ("_route" if route is not None else ""),
    )(*args)
    return out if route is not None else out[0]


def _swiglu_step(x_ref, gpre_ref, wg_ref, wu_ref, wd_ref, hn_ref, acc_ref):
    j = pl.program_id(1)

    @pl.when(j == 0)
    def _():
        if gpre_ref is not None:
            hn_ref[...] = _rms(x_ref[...], gpre_ref[...]).astype(BF16)
        acc_ref[...] = jnp.zeros_like(acc_ref)

    hn = x_ref[...].astype(BF16) if gpre_ref is None else hn_ref[...]
    gate = jnp.dot(hn, wg_ref[...].astype(BF16), preferred_element_type=F32)
    up = jnp.dot(hn, wu_ref[...].astype(BF16), preferred_element_type=F32)
    act = (gate * jax.nn.sigmoid(gate) * up).astype(BF16)
    acc_ref[...] += jnp.dot(act, wd_ref[...].astype(BF16), preferred_element_type=F32)


def _ffn_kernel(x_ref, gpre_ref, wg_ref, wu_ref, wd_ref, gpost_ref, o_ref, hn_ref, acc_ref):
    _swiglu_step(x_ref, gpre_ref, wg_ref, wu_ref, wd_ref, hn_ref, acc_ref)

    @pl.when(pl.program_id(1) == pl.num_programs(1) - 1)
    def _():
        o_ref[...] = x_ref[...] + _rms(acc_ref[...], gpost_ref[...])


def _ffn_layer(x, gpre, w_gu, w_down, gpost, *, tm, fb):
    n, d = x.shape
    f = w_down.shape[0]
    nf = f // fb
    tok = pl.BlockSpec((tm, d), lambda i, j: (i, 0))
    return pl.pallas_call(
        _ffn_kernel,
        grid=(n // tm, nf),
        in_specs=[
            tok, _full((1, d)),
            pl.BlockSpec((d, fb), lambda i, j: (0, j)),
            pl.BlockSpec((d, fb), lambda i, j: (0, nf + j)),
            pl.BlockSpec((fb, d), lambda i, j: (j, 0)),
            _full((1, d)),
        ],
        out_specs=tok,
        out_shape=jax.ShapeDtypeStruct((n, d), F32),
        scratch_shapes=[pltpu.VMEM((tm, d), BF16), pltpu.VMEM((tm, d), F32)],
        compiler_params=_params("arbitrary", "arbitrary"),
        name="dense_swiglu",
    )(x, _row(gpre), w_gu, w_gu, w_down, _row(gpost))


def _experts_kernel(tile_expert_ref, ntiles_ref, x_ref, wg_ref, wu_ref, wd_ref, y_ref):
    i = pl.program_id(0)
    last = pl.program_id(1) == pl.num_programs(1) - 1
    live = i < ntiles_ref[0]

    @pl.when(live)
    def _():
        _swiglu_step(x_ref, None, wg_ref.at[0], wu_ref.at[0], wd_ref.at[0], None, y_ref)

    @pl.when(jnp.logical_not(live) & last)
    def _():
        y_ref[...] = jnp.zeros_like(y_ref)


def _experts_layer(xs, w_gu, w_down, tile_expert, ntiles, *, te, fb):
    r, d = xs.shape
    _, f, _ = w_down.shape
    nf = f // fb

    def x_map(i, j, te_ref, nt_ref):
        return (jnp.minimum(i, nt_ref[0] - 1), 0)

    def fblock(i, j, nt_ref):
        return jnp.where(i < nt_ref[0], j, nf - 1)

    def expert(i, te_ref, nt_ref):
        return te_ref[jnp.minimum(i, nt_ref[0] - 1)]

    grid_spec = pltpu.PrefetchScalarGridSpec(
        num_scalar_prefetch=2,
        grid=(r // te, nf),
        in_specs=[
            pl.BlockSpec((te, d), x_map),
            pl.BlockSpec((1, d, fb), lambda i, j, te_ref, nt_ref: (expert(i, te_ref, nt_ref), 0, fblock(i, j, nt_ref))),
            pl.BlockSpec((1, d, fb),
                         lambda i, j, te_ref, nt_ref: (expert(i, te_ref, nt_ref), 0, nf + fblock(i, j, nt_ref))),
            pl.BlockSpec((1, fb, d), lambda i, j, te_ref, nt_ref: (expert(i, te_ref, nt_ref), fblock(i, j, nt_ref), 0)),
        ],
        out_specs=pl.BlockSpec((te, d), lambda i, j, te_ref, nt_ref: (i, 0)),
    )
    return pl.pallas_call(
        _experts_kernel,
        grid_spec=grid_spec,
        out_shape=jax.ShapeDtypeStruct((r, d), F32),
        compiler_params=_params("arbitrary", "arbitrary"),
        name="moe_experts",
    )(tile_expert, ntiles, xs, w_gu, w_gu, w_down)


def _conv_kernel(x_ref, gpre_ref, w1_ref, b1_ref, dw_ref, dwb_ref, lng_ref, lnb_ref, w2_ref, b2_ref, gpost_ref,
                 o_ref, uext_ref, v_ref, *, tiles_per_seq, width):
    tm, d = x_ref.shape
    nblk = d // LANES
    group = 2
    t = pl.program_id(0) % tiles_per_seq

    @pl.when(t == 0)
    def _():
        uext_ref[:, 0:CONV_HALO, :] = jnp.zeros((nblk, CONV_HALO, LANES), F32)

    x = x_ref[...]
    h = _rms(x, gpre_ref[...]).astype(BF16)
    first = CONV_HALO - (width - 1)
    for g0 in range(0, nblk, group):
        lo, hi = g0 * LANES, (g0 + group) * LANES
        val = jnp.dot(h, w1_ref[:, lo:hi], preferred_element_type=F32) + b1_ref[:, lo:hi]
        gate = jnp.dot(h, w1_ref[:, d + lo:d + hi], preferred_element_type=F32) + b1_ref[:, d + lo:d + hi]
        u = val * jax.nn.sigmoid(gate)
        for c in range(g0, g0 + group):
            uext_ref[c, CONV_HALO:, :] = u[:, (c - g0) * LANES:(c - g0 + 1) * LANES]
            taps = [dw_ref[c, k:k + 1, :] for k in range(width)]
            for r0 in range(0, tm, CONV_ROWS):
                acc = taps[0] * uext_ref[c, r0 + first:r0 + first + CONV_ROWS, :]
                for k in range(1, width):
                    acc = acc + taps[k] * uext_ref[c, r0 + first + k:r0 + first + k + CONV_ROWS, :]
                v_ref[c, r0:r0 + CONV_ROWS, :] = acc
            uext_ref[c, 0:CONV_HALO, :] = uext_ref[c, tm:tm + CONV_HALO, :]

    v = jnp.concatenate([v_ref[c] for c in range(nblk)], axis=-1) + dwb_ref[...]
    mu = jnp.mean(v, axis=-1, keepdims=True)
    vc = v - mu
    var = jnp.mean(vc * vc, axis=-1, keepdims=True)
    z = vc * lax.rsqrt(var + EPS) * lng_ref[...] + lnb_ref[...]
    z = (z * jax.nn.sigmoid(z)).astype(BF16)
    y = jnp.dot(z, w2_ref[...], preferred_element_type=F32) + b2_ref[...]
    o_ref[...] = x + _rms(y, gpost_ref[...])


def _conv_layer(x, gpre, pw1_w, pw1_b, dw_w, dw_b, ln_g, ln_b, pw2_w, pw2_b, gpost, *, seq, tm):
    n, d = x.shape
    width = dw_w.shape[0]
    nblk = d // LANES
    kpad = -(-width // SUBLANES) * SUBLANES
    dw = jnp.pad(dw_w, ((0, kpad - width), (0, 0))).reshape(kpad, nblk, LANES).transpose(1, 0, 2)
    tok = pl.BlockSpec((tm, d), lambda i: (i, 0))
    vec = _full((1, d))
    return pl.pallas_call(
        functools.partial(_conv_kernel, tiles_per_seq=seq // tm, width=width),
        grid=(n // tm,),
        in_specs=[tok, vec, _full((d, 2 * d)), _full((1, 2 * d)), _full((nblk, kpad, LANES)),
                  vec, vec, vec, _full((d, d)), vec, vec],
        out_specs=tok,
        out_shape=jax.ShapeDtypeStruct((n, d), F32),
        scratch_shapes=[pltpu.VMEM((nblk, tm + CONV_HALO, LANES), F32), pltpu.VMEM((nblk, tm, LANES), F32)],
        compiler_params=_params("arbitrary"),
        name="conv_module",
    )(x, _row(gpre), pw1_w.astype(BF16), _row(pw1_b), dw, _row(dw_b), _row(ln_g), _row(ln_b),
      pw2_w.astype(BF16), _row(pw2_b), _row(gpost))


def _chunk_pieces(tbl_ref, tile, n_experts, max_piece, visit):
    for e in range(n_experts):
        base, length, off = tbl_ref[tile, e], tbl_ref[tile, n_experts + e], tbl_ref[tile, 2 * n_experts + e]
        p = max_piece
        while p >= ROW_ALIGN:
            done = jnp.bitwise_and(length, ~(2 * p - 1))

            @pl.when(jnp.bitwise_and(length, p) != 0)
            def _(p=p, done=done):
                visit(p, pl.multiple_of(off + done, ROW_ALIGN), pl.multiple_of(base + done, ROW_ALIGN))

            p //= 2


def _dispatch_kernel(tbl_ref, fill_start_ref, fill_len_ref, x_ref, gpre_ref, metat_ref, xs_ref, xp_ref, zero_ref,
                     sem, zsem, *, n_experts):
    tm, d = x_ref.shape
    rows = xp_ref.shape[1]
    i = pl.program_id(0)
    last = pl.num_programs(0) - 1
    slot = i % 2

    hn = _rms(x_ref[...], gpre_ref[...]).astype(BF16)
    row_id = lax.broadcasted_iota(jnp.int32, (rows, tm), 0)
    place = metat_ref[TOP_K:2 * TOP_K, :].astype(jnp.int32)
    hit = (row_id == place[0:1, :]) | (row_id == place[1:2, :])
    perm = jnp.where(hit, 1.0, 0.0).astype(BF16)
    xp_ref[slot] = jnp.dot(perm, hn, preferred_element_type=F32)

    def chunk_copy(buf, p, tile_row, global_row):
        return pltpu.make_async_copy(xp_ref.at[buf, pl.ds(tile_row, p), :], xs_ref.at[pl.ds(global_row, p), :],
                                     sem.at[buf])

    _chunk_pieces(tbl_ref, i, n_experts, tm, lambda *a: chunk_copy(slot, *a).start())

    @pl.when(i > 0)
    def _():
        _chunk_pieces(tbl_ref, i - 1, n_experts, tm, lambda *a: chunk_copy(1 - slot, *a).wait())

    @pl.when(i == last)
    def _():
        zero_ref[...] = jnp.zeros_like(zero_ref)
        zrows = zero_ref.shape[0]
        for e in range(n_experts + 1):
            step = ROW_ALIGN if e < n_experts else zrows

            def zero_copy(r, e=e, step=step):
                row0 = pl.multiple_of(fill_start_ref[e] + r * step, step)
                return pltpu.make_async_copy(zero_ref.at[pl.ds(0, step), :], xs_ref.at[pl.ds(row0, step), :], zsem)

            lax.fori_loop(0, fill_len_ref[e], lambda r, c: (zero_copy(r).start(), c)[1], 0)
            lax.fori_loop(0, fill_len_ref[e], lambda r, c: (zero_copy(r).wait(), c)[1], 0)
        _chunk_pieces(tbl_ref, i, n_experts, tm, lambda *a: chunk_copy(slot, *a).wait())


def _dispatch(x, gpre, tbl, metat, fill_start, fill_len, *, rows, tm, tile_rows, zrows, n_experts):
    n, d = x.shape
    smem = pl.BlockSpec(memory_space=pltpu.SMEM)
    return pl.pallas_call(
        functools.partial(_dispatch_kernel, n_experts=n_experts),
        grid=(n // tm,),
        in_specs=[smem, smem, smem, pl.BlockSpec((tm, d), lambda i: (i, 0)), _full((1, d)),
                  pl.BlockSpec((SUBLANES, tm), lambda i: (i, 0))],
        out_specs=pl.BlockSpec(memory_space=pl.ANY),
        out_shape=jax.ShapeDtypeStruct((rows, d), F32),
        scratch_shapes=[pltpu.VMEM((2, tile_rows, d), F32), pltpu.VMEM((zrows, d), F32),
                        pltpu.SemaphoreType.DMA((2,)), pltpu.SemaphoreType.DMA],
        compiler_params=_params("arbitrary"),
        name="moe_dispatch",
    )(tbl, fill_start, fill_len, x, _row(gpre), metat)


def _combine_kernel(tbl_ref, x_ref, meta_ref, ys_ref, gpost_ref, o_ref, yp_ref, sem, *, n_experts):
    tm, d = x_ref.shape
    rows = yp_ref.shape[1]
    i = pl.program_id(0)
    slot = i % 2

    def chunk_copy(buf, p, tile_row, global_row):
        return pltpu.make_async_copy(ys_ref.at[pl.ds(global_row, p), :], yp_ref.at[buf, pl.ds(tile_row, p), :],
                                     sem.at[buf])

    @pl.when(i == 0)
    def _():
        yp_ref[...] = jnp.zeros_like(yp_ref)
        _chunk_pieces(tbl_ref, i, n_experts, tm, lambda *a: chunk_copy(slot, *a).start())

    @pl.when(i + 1 < pl.num_programs(0))
    def _():
        _chunk_pieces(tbl_ref, i + 1, n_experts, tm, lambda *a: chunk_copy(1 - slot, *a).start())

    _chunk_pieces(tbl_ref, i, n_experts, tm, lambda *a: chunk_copy(slot, *a).wait())

    yp = yp_ref[slot].astype(BF16)
    row_id = lax.broadcasted_iota(jnp.int32, (tm, rows), 1)
    meta = meta_ref[...]
    place = meta[:, TOP_K:2 * TOP_K].astype(jnp.int32)
    y = None
    for k in range(TOP_K):
        pick = jnp.where(row_id == place[:, k:k + 1], 1.0, 0.0).astype(BF16)
        yk = jnp.dot(pick, yp, preferred_element_type=F32) * meta[:, 2 * TOP_K + k:2 * TOP_K + k + 1]
        y = yk if y is None else y + yk
    o_ref[...] = x_ref[...] + _rms(y, gpost_ref[...])


def _combine(x, tbl, meta, ys, gpost, *, tm, tile_rows, n_experts):
    n, d = x.shape
    tok = pl.BlockSpec((tm, d), lambda i: (i, 0))
    return pl.pallas_call(
        functools.partial(_combine_kernel, n_experts=n_experts),
        grid=(n // tm,),
        in_specs=[
            pl.BlockSpec(memory_space=pltpu.SMEM),
            tok,
            pl.BlockSpec((tm, SUBLANES), lambda i: (i, 0)),
            pl.BlockSpec(memory_space=pl.ANY),
            _full((1, d)),
        ],
        out_specs=tok,
        out_shape=jax.ShapeDtypeStruct((n, d), F32),
        scratch_shapes=[pltpu.VMEM((2, tile_rows, d), F32), pltpu.SemaphoreType.DMA((2,))],
        compiler_params=_params("arbitrary"),
        name="moe_combine",
    )(tbl, x, meta, ys, _row(gpost))


def _moe_layer(x, meta, metat, cnt, gpre, w_gu, w_down, gpost, *, tm, te, fb):
    n, d = x.shape
    e = w_gu.shape[0]
    nt = n // tm

    counts = cnt.reshape(nt, SUBLANES, LANES)[:, :e, 0].astype(jnp.int32)
    chunk = (counts + ROW_ALIGN - 1) // ROW_ALIGN * ROW_ALIGN
    tile_off = jnp.cumsum(chunk, axis=1) - chunk
    total = jnp.sum(chunk, axis=0)
    padded = (total + te - 1) // te * te
    ends = jnp.cumsum(padded)
    starts = ends - padded
    base = starts[None, :] + jnp.cumsum(chunk, axis=0) - chunk
    tbl = jnp.concatenate([base, chunk, tile_off], axis=1)
    tile_rows = -(-(TOP_K * tm + e * ROW_ALIGN) // LANES) * LANES

    rows = TOP_K * n + e * ROW_ALIGN * nt + e * te
    rows = -(-rows // te) * te
    ntiles = (ends[-1] // te).reshape(1)
    tile_ids = jnp.arange(rows // te, dtype=jnp.int32)
    tile_expert = jnp.minimum(jnp.sum(tile_ids[:, None] >= (ends // te)[None, :], axis=1), e - 1).astype(jnp.int32)

    zrows = min(te, ZERO_ROWS)
    assert te % zrows == 0 and zrows % ROW_ALIGN == 0
    fill_start = jnp.concatenate([starts + total, ends[-1:]])
    fill_len = jnp.concatenate([(padded - total) // ROW_ALIGN, (rows - ends[-1:]) // zrows])
    xs = _dispatch(x, gpre, tbl, metat, fill_start, fill_len, rows=rows, tm=tm, tile_rows=tile_rows, zrows=zrows,
                   n_experts=e)
    ys = _experts_layer(xs, w_gu, w_down, tile_expert, ntiles, te=te, fb=fb)
    return _combine(x, tbl, meta, ys, gpost, tm=tm, tile_rows=tile_rows, n_experts=e)


def _tile(n, want):
    t = min(n, want)
    while n % t:
        t //= 2
    return t


@jax.jit
def kernel(x, mem, mem_norm_g, mix_pre_g, mix_post_g, xa_pre_g, xa_post_g, ffn_pre_g, ffn_post_g, xa_wq, xa_wkv, xa_wo, pool_w, pool_b, pool_scale, conv_pw1_w, conv_pw1_b, conv_dw_w, conv_dw_b, conv_ln_g, conv_ln_b, conv_pw2_w, conv_pw2_b, ffn_w_gu, ffn_w_down, moe_router, moe_w_gu, moe_w_down):
    b, s, d = x.shape
    depth = mix_pre_g.shape[0]
    f = ffn_w_down.shape[1]
    tm = _tile(s, 512)
    tf = _tile(s, 1024)
    fb = _tile(f, 512)
    kt, v = _memory_kv(mem, mem_norm_g, xa_wkv)
    h = x.reshape(b * s, d)
    for i in range(depth):
        j = i // 2
        xattn = functools.partial(_xattn_layer, kt=kt[i], v=v[i], wq=xa_wq[i], wo=xa_wo[i], gpre=xa_pre_g[i],
                                  gpost=xa_post_g[i], seq=s)
        if i % 2 == 0:
            h = xattn(h, pool=(mix_pre_g[i], pool_w[j], pool_b[j], pool_scale[j], mix_post_g[i]), tm=tf)
            h = _ffn_layer(h, ffn_pre_g[i], ffn_w_gu[j], ffn_w_down[j], ffn_post_g[i], tm=tf, fb=fb)
        else:
            h = _conv_layer(h, mix_pre_g[i], conv_pw1_w[j], conv_pw1_b[j], conv_dw_w[j], conv_dw_b[j], conv_ln_g[j],
                            conv_ln_b[j], conv_pw2_w[j], conv_pw2_b[j], mix_post_g[i], seq=s, tm=tm)
            h, meta, metat, cnt = xattn(h, route=(ffn_pre_g[i], moe_router[j]), tm=tm)
            h = _moe_layer(h, meta, metat, cnt, ffn_pre_g[i], moe_w_gu[j], moe_w_down[j], ffn_post_g[i], tm=tm, te=tf, fb=fb)
    return h.reshape(b, s, d)
```

```python
import functools

import jax
import jax.numpy as jnp
from jax import lax
from jax.experimental import pallas as pl
from jax.experimental.pallas import tpu as pltpu

EPS = 1e-6
N_XA_HEADS = 4
POOL_WINDOWS = (2, 4, 8, 16)
TOP_K = 2

LANES = 128
SUBLANES = 8
POOL_HALO = 16
CONV_HALO = 32
CONV_ROWS = 64
CONV_GROUP_BLOCKS = 2
POOL_XATTN_CHAINS = 4
ROW_ALIGN = SUBLANES
ZERO_ROWS = 256
VMEM_LIMIT = 56 * 1024 * 1024

F32 = jnp.float32
BF16 = jnp.bfloat16


def _rms(x, g):
    return x * lax.rsqrt(jnp.mean(x * x, axis=-1, keepdims=True) + EPS) * g


def _params(*sem):
    return pltpu.CompilerParams(dimension_semantics=sem, vmem_limit_bytes=VMEM_LIMIT)


def _row(v):
    return v.reshape(1, -1).astype(F32)


def _full(shape):
    nd = len(shape)
    return pl.BlockSpec(shape, lambda *_: (0,) * nd)


def _kv_kernel(mem_ref, g_ref, wkv_ref, kt_ref, v_ref):
    d = mem_ref.shape[-1]
    mn = _rms(mem_ref[0], g_ref[...]).astype(BF16)
    kv = jnp.dot(mn, wkv_ref[0], preferred_element_type=F32)
    kt_ref[0, 0] = kv[:, :d].T.astype(BF16)
    v_ref[0, 0] = kv[:, d:].astype(BF16)


def _memory_kv(mem, mem_norm_g, wkv):
    depth, d, _ = wkv.shape
    b, m, _ = mem.shape
    return pl.pallas_call(
        _kv_kernel,
        grid=(depth, b),
        in_specs=[
            pl.BlockSpec((1, m, d), lambda l, i: (i, 0, 0)),
            _full((1, d)),
            pl.BlockSpec((1, d, 2 * d), lambda l, i: (l, 0, 0)),
        ],
        out_specs=[
            pl.BlockSpec((1, 1, d, m), lambda l, i: (l, i, 0, 0)),
            pl.BlockSpec((1, 1, m, d), lambda l, i: (l, i, 0, 0)),
        ],
        out_shape=[
            jax.ShapeDtypeStruct((depth, b, d, m), BF16),
            jax.ShapeDtypeStruct((depth, b, m, d), BF16),
        ],
        compiler_params=_params("arbitrary", "arbitrary"),
        name="memory_kv",
    )(mem, _row(mem_norm_g), wkv.astype(BF16))


def _pool_tile(x, gpre_ref, w_ref, b_ref, scale_ref, gpost_ref, ext_ref, tiles_per_seq):
    tm, d = x.shape
    c = d // len(POOL_WINDOWS)
    t = pl.program_id(0) % tiles_per_seq

    @pl.when(t == 0)
    def _():
        ext_ref[0:POOL_HALO, :] = jnp.zeros((POOL_HALO, d), F32)

    hn = _rms(x, gpre_ref[...])
    ext_ref[POOL_HALO:, :] = hn

    pos = (t * tm + lax.broadcasted_iota(jnp.int32, (tm, 1), 0) + 1).astype(F32)
    ys = []
    for g, w in enumerate(POOL_WINDOWS):
        s = ext_ref[:, g * c:(g + 1) * c]
        span = 1
        while span < w:
            s = s + pltpu.roll(s, span, 0)
            span *= 2
        pooled = s[POOL_HALO:, :] / jnp.minimum(pos, float(w)) - hn[:, g * c:(g + 1) * c]
        y = jnp.dot(pooled.astype(BF16), w_ref[g], preferred_element_type=F32)
        ys.append(y + b_ref[:, g * c:(g + 1) * c])
    y = jnp.concatenate(ys, axis=-1) * scale_ref[...]
    ext_ref[0:POOL_HALO, :] = hn[tm - POOL_HALO:, :]
    return x + _rms(y, gpost_ref[...])


def _xattn_tile(x, kt_ref, v_ref, wq_ref, wo_ref, gpre_ref, gpost_ref, chains=1):
    tm, d = x.shape
    hd = d // N_XA_HEADS
    sub = tm // chains
    xs = [x[c * sub:(c + 1) * sub] for c in range(chains)]
    hs = [_rms(xc, gpre_ref[...]).astype(BF16) for xc in xs]
    qs = [jnp.dot(h, wq_ref[...], preferred_element_type=F32) * (hd ** -0.5) for h in hs]
    heads = [[] for _ in range(chains)]
    for i in range(N_XA_HEADS):
        sl = slice(i * hd, (i + 1) * hd)
        ss = [jnp.dot(q[:, sl].astype(BF16), kt_ref[0, sl, :], preferred_element_type=F32) for q in qs]
        ps = []
        for s in ss:
            p = jnp.exp(s - jnp.max(s, axis=-1, keepdims=True))
            ps.append((p / jnp.sum(p, axis=-1, keepdims=True)).astype(BF16))
        for c, p in enumerate(ps):
            heads[c].append(jnp.dot(p, v_ref[0, :, sl], preferred_element_type=F32).astype(BF16))
    ys = [jnp.dot(jnp.concatenate(hc, axis=-1), wo_ref[...], preferred_element_type=F32) for hc in heads]
    return jnp.concatenate([xc + _rms(y, gpost_ref[...]) for xc, y in zip(xs, ys)], axis=0)


def _route_tile(x, gffn_ref, wr_ref, meta_ref, metat_ref, cnt_ref, upper_ref, n_experts):
    tm = x.shape[0]

    @pl.when(pl.program_id(0) == 0)
    def _():
        before = lax.broadcasted_iota(jnp.int32, (tm, tm), 0) < lax.broadcasted_iota(jnp.int32, (tm, tm), 1)
        upper_ref[...] = jnp.where(before, 1.0, 0.0).astype(BF16)

    hn = _rms(x, gffn_ref[...]).astype(BF16)
    logits = jnp.dot(hn, wr_ref[...], preferred_element_type=F32)
    lg = logits.T[0:SUBLANES, :]
    sub = lax.broadcasted_iota(jnp.int32, (SUBLANES, tm), 0).astype(F32)
    lg = jnp.where(sub < n_experts, lg, -jnp.inf)
    m1 = jnp.max(lg, axis=0, keepdims=True)
    e1 = jnp.min(jnp.where(lg == m1, sub, float(SUBLANES)), axis=0, keepdims=True)
    rest = jnp.where(sub == e1, -jnp.inf, lg)
    m2 = jnp.max(rest, axis=0, keepdims=True)
    e2 = jnp.min(jnp.where(rest == m2, sub, float(SUBLANES)), axis=0, keepdims=True)
    ex = jnp.exp(m2 - m1)
    w1 = 1.0 / (1.0 + ex)
    w2 = ex / (1.0 + ex)

    hit1 = sub == e1
    hit2 = sub == e2
    onehot = jnp.where(hit1 | hit2, 1.0, 0.0)
    earlier = jnp.dot(onehot.astype(BF16), upper_ref[...], preferred_element_type=F32)
    counts = jnp.sum(onehot, axis=1, keepdims=True)
    cnt_ref[...] = jnp.broadcast_to(counts, cnt_ref.shape)
    chunk = jnp.ceil(counts * (1.0 / ROW_ALIGN)) * ROW_ALIGN
    lower = (lax.broadcasted_iota(jnp.int32, (SUBLANES, SUBLANES), 1)
             < lax.broadcasted_iota(jnp.int32, (SUBLANES, SUBLANES), 0))
    chunk_start = jnp.dot(jnp.where(lower, 1.0, 0.0).astype(BF16), jnp.broadcast_to(chunk, (SUBLANES, LANES)).astype(BF16),
                          preferred_element_type=F32)[:, 0:1]
    place = chunk_start + earlier
    r1 = jnp.sum(jnp.where(hit1, place, 0.0), axis=0, keepdims=True)
    r2 = jnp.sum(jnp.where(hit2, place, 0.0), axis=0, keepdims=True)

    rows = jnp.zeros((SUBLANES, tm), F32)
    for k, val in enumerate((e1, e2, r1, r2, w1, w2)):
        rows = jnp.where(sub == float(k), val, rows)
    metat_ref[...] = rows
    cols = jnp.concatenate([rows, jnp.zeros((LANES - SUBLANES, tm), F32)], axis=0).T
    meta_ref[...] = cols[:, :meta_ref.shape[1]]


def _xattn_kernel(*refs, tiles_per_seq, pool, n_experts):
    it = iter(refs)

    def take(k):
        return [next(it) for _ in range(k)]

    (x_ref,) = take(1)
    pool_in = take(5) if pool else None
    attn_in = take(6)
    route_in = take(2) if n_experts else None
    (o_ref,) = take(1)
    route_out = take(3) if n_experts else None
    (ext_ref,) = take(1) if pool else (None,)
    (upper_ref,) = take(1) if n_experts else (None,)

    x = x_ref[...]
    if pool:
        x = _pool_tile(x, *pool_in, ext_ref, tiles_per_seq)
    chains = POOL_XATTN_CHAINS if pool and x.shape[0] % (POOL_XATTN_CHAINS * SUBLANES) == 0 else 1
    xn = _xattn_tile(x, *attn_in, chains=chains)
    if n_experts:
        _route_tile(xn, *route_in, *route_out, upper_ref, n_experts)
    o_ref[...] = xn


def _xattn_layer(x, *, kt, v, wq, wo, gpre, gpost, seq, tm, pool=None, route=None):
    n, d = x.shape
    m = kt.shape[-1]
    tps = seq // tm
    tok = pl.BlockSpec((tm, d), lambda i: (i, 0))
    vec = _full((1, d))
    in_specs, args, scratch = [tok], [x], []
    if pool is not None:
        pg, w_grp, b_grp, scale, ppost = pool
        in_specs += [vec, _full(w_grp.shape), vec, vec, vec]
        args += [_row(pg), w_grp.astype(BF16), _row(b_grp), _row(scale), _row(ppost)]
        scratch.append(pltpu.VMEM((tm + POOL_HALO, d), F32))
    in_specs += [pl.BlockSpec((1, d, m), lambda i: (i // tps, 0, 0)), pl.BlockSpec((1, m, d), lambda i: (i // tps, 0, 0)),
                 _full((d, d)), _full((d, d)), vec, vec]
    args += [kt, v, wq.astype(BF16), wo.astype(BF16), _row(gpre), _row(gpost)]
    out_specs, out_shape, n_experts = [tok], [jax.ShapeDtypeStruct((n, d), F32)], 0
    if route is not None:
        gffn, w_router = route
        n_experts = w_router.shape[1]
        assert n_experts <= SUBLANES
        in_specs += [vec, _full((d, LANES))]
        args += [_row(gffn), jnp.pad(w_router, ((0, 0), (0, LANES - n_experts))).astype(BF16)]
        out_specs += [pl.BlockSpec((tm, SUBLANES), lambda i: (i, 0)), pl.BlockSpec((SUBLANES, tm), lambda i: (i, 0)),
                      pl.BlockSpec((SUBLANES, LANES), lambda i: (i, 0))]
        out_shape += [jax.ShapeDtypeStruct((n, SUBLANES), F32), jax.ShapeDtypeStruct((n // tm * SUBLANES, tm), F32),
                      jax.ShapeDtypeStruct((n // tm * SUBLANES, LANES), F32)]
        scratch.append(pltpu.VMEM((tm, tm), BF16))
    out = pl.pallas_call(
        functools.partial(_xattn_kernel, tiles_per_seq=tps, pool=pool is not None, n_experts=n_experts),
        grid=(n // tm,),
        in_specs=in_specs,
        out_specs=out_specs,
        out_shape=out_shape,
        scratch_shapes=scratch,
        compiler_params=_params("arbitrary"),
        name="xattn" + ("_pool" if pool is not None else "") + ("_route" if route is not None else ""),
    )(*args)
    return out if route is not None else out[0]


def _swiglu_step(x_ref, gpre_ref, wg_ref, wu_ref, wd_ref, hn_ref, acc_ref):
    j = pl.program_id(1)

    @pl.when(j == 0)
    def _():
        if gpre_ref is not None:
            hn_ref[...] = _rms(x_ref[...], gpre_ref[...]).astype(BF16)
        acc_ref[...] = jnp.zeros_like(acc_ref)

    hn = x_ref[...].astype(BF16) if gpre_ref is None else hn_ref[...]
    gate = jnp.dot(hn, wg_ref[...].astype(BF16), preferred_element_type=F32)
    up = jnp.dot(hn, wu_ref[...].astype(BF16), preferred_element_type=F32)
    act = (gate * jax.nn.sigmoid(gate) * up).astype(BF16)
    acc_ref[...] += jnp.dot(act, wd_ref[...].astype(BF16), preferred_element_type=F32)


def _ffn_kernel(x_ref, gpre_ref, wg_ref, wu_ref, wd_ref, gpost_ref, o_ref, hn_ref, acc_ref):
    _swiglu_step(x_ref, gpre_ref, wg_ref, wu_ref, wd_ref, hn_ref, acc_ref)

    @pl.when(pl.program_id(1) == pl.num_programs(1) - 1)
    def _():
        o_ref[...] = x_ref[...] + _rms(acc_ref[...], gpost_ref[...])


def _ffn_layer(x, gpre, w_gu, w_down, gpost, *, tm, fb):
    n, d = x.shape
    f = w_down.shape[0]
    nf = f // fb
    tok = pl.BlockSpec((tm, d), lambda i, j: (i, 0))
    return pl.pallas_call(
        _ffn_kernel,
        grid=(n // tm, nf),
        in_specs=[
            tok, _full((1, d)),
            pl.BlockSpec((d, fb), lambda i, j: (0, j)),
            pl.BlockSpec((d, fb), lambda i, j: (0, nf + j)),
            pl.BlockSpec((fb, d), lambda i, j: (j, 0)),
            _full((1, d)),
        ],
        out_specs=tok,
        out_shape=jax.ShapeDtypeStruct((n, d), F32),
        scratch_shapes=[pltpu.VMEM((tm, d), BF16), pltpu.VMEM((tm, d), F32)],
        compiler_params=_params("arbitrary", "arbitrary"),
        name="dense_swiglu",
    )(x, _row(gpre), w_gu, w_gu, w_down, _row(gpost))


def _experts_kernel(tile_expert_ref, ntiles_ref, x_ref, wg_ref, wu_ref, wd_ref, y_ref):
    i = pl.program_id(0)
    last = pl.program_id(1) == pl.num_programs(1) - 1
    live = i < ntiles_ref[0]

    @pl.when(live)
    def _():
        _swiglu_step(x_ref, None, wg_ref.at[0], wu_ref.at[0], wd_ref.at[0], None, y_ref)

    @pl.when(jnp.logical_not(live) & last)
    def _():
        y_ref[...] = jnp.zeros_like(y_ref)


def _experts_layer(xs, w_gu, w_down, tile_expert, ntiles, *, te, fb):
    r, d = xs.shape
    _, f, _ = w_down.shape
    nf = f // fb

    def x_map(i, j, te_ref, nt_ref):
        return (jnp.minimum(i, nt_ref[0] - 1), 0)

    def fblock(i, j, nt_ref):
        return jnp.where(i < nt_ref[0], j, nf - 1)

    def expert(i, te_ref, nt_ref):
        return te_ref[jnp.minimum(i, nt_ref[0] - 1)]

    grid_spec = pltpu.PrefetchScalarGridSpec(
        num_scalar_prefetch=2,
        grid=(r // te, nf),
        in_specs=[
            pl.BlockSpec((te, d), x_map),
            pl.BlockSpec((1, d, fb), lambda i, j, te_ref, nt_ref: (expert(i, te_ref, nt_ref), 0, fblock(i, j, nt_ref))),
            pl.BlockSpec((1, d, fb),
                         lambda i, j, te_ref, nt_ref: (expert(i, te_ref, nt_ref), 0, nf + fblock(i, j, nt_ref))),
            pl.BlockSpec((1, fb, d), lambda i, j, te_ref, nt_ref: (expert(i, te_ref, nt_ref), fblock(i, j, nt_ref), 0)),
        ],
        out_specs=pl.BlockSpec((te, d), lambda i, j, te_ref, nt_ref: (i, 0)),
    )
    return pl.pallas_call(
        _experts_kernel,
        grid_spec=grid_spec,
        out_shape=jax.ShapeDtypeStruct((r, d), F32),
        compiler_params=_params("arbitrary", "arbitrary"),
        name="moe_experts",
    )(tile_expert, ntiles, xs, w_gu, w_gu, w_down)


def _conv_kernel(x_ref, gpre_ref, w1_ref, b1_ref, dw_ref, dwb_ref, lng_ref, lnb_ref, w2_ref, b2_ref, gpost_ref,
                 o_ref, uext_ref, v_ref, *, tiles_per_seq, width):
    tm, d = x_ref.shape
    nblk = d // LANES
    group = CONV_GROUP_BLOCKS
    t = pl.program_id(0) % tiles_per_seq

    @pl.when(t == 0)
    def _():
        uext_ref[:, 0:CONV_HALO, :] = jnp.zeros((nblk, CONV_HALO, LANES), F32)

    x = x_ref[...]
    h = _rms(x, gpre_ref[...]).astype(BF16)
    first = CONV_HALO - (width - 1)
    for g0 in range(0, nblk, group):
        lo, hi = g0 * LANES, (g0 + group) * LANES
        val = jnp.dot(h, w1_ref[:, lo:hi], preferred_element_type=F32) + b1_ref[:, lo:hi]
        gate = jnp.dot(h, w1_ref[:, d + lo:d + hi], preferred_element_type=F32) + b1_ref[:, d + lo:d + hi]
        u = val * jax.nn.sigmoid(gate)
        for c in range(g0, g0 + group):
            uext_ref[c, CONV_HALO:, :] = u[:, (c - g0) * LANES:(c - g0 + 1) * LANES]
            taps = [dw_ref[c, k:k + 1, :] for k in range(width)]
            for r0 in range(0, tm, CONV_ROWS):
                acc = taps[0] * uext_ref[c, r0 + first:r0 + first + CONV_ROWS, :]
                for k in range(1, width):
                    acc = acc + taps[k] * uext_ref[c, r0 + first + k:r0 + first + k + CONV_ROWS, :]
                v_ref[c, r0:r0 + CONV_ROWS, :] = acc
            uext_ref[c, 0:CONV_HALO, :] = uext_ref[c, tm:tm + CONV_HALO, :]

    v = jnp.concatenate([v_ref[c] for c in range(nblk)], axis=-1) + dwb_ref[...]
    mu = jnp.mean(v, axis=-1, keepdims=True)
    vc = v - mu
    var = jnp.mean(vc * vc, axis=-1, keepdims=True)
    z = vc * lax.rsqrt(var + EPS) * lng_ref[...] + lnb_ref[...]
    z = (z * jax.nn.sigmoid(z)).astype(BF16)
    y = jnp.dot(z, w2_ref[...], preferred_element_type=F32) + b2_ref[...]
    o_ref[...] = x + _rms(y, gpost_ref[...])


def _conv_layer(x, gpre, pw1_w, pw1_b, dw_w, dw_b, ln_g, ln_b, pw2_w, pw2_b, gpost, *, seq, tm):
    n, d = x.shape
    width = dw_w.shape[0]
    nblk = d // LANES
    kpad = -(-width // SUBLANES) * SUBLANES
    dw = jnp.pad(dw_w, ((0, kpad - width), (0, 0))).reshape(kpad, nblk, LANES).transpose(1, 0, 2)
    tok = pl.BlockSpec((tm, d), lambda i: (i, 0))
    vec = _full((1, d))
    return pl.pallas_call(
        functools.partial(_conv_kernel, tiles_per_seq=seq // tm, width=width),
        grid=(n // tm,),
        in_specs=[tok, vec, _full((d, 2 * d)), _full((1, 2 * d)), _full((nblk, kpad, LANES)),
                  vec, vec, vec, _full((d, d)), vec, vec],
        out_specs=tok,
        out_shape=jax.ShapeDtypeStruct((n, d), F32),
        scratch_shapes=[pltpu.VMEM((nblk, tm + CONV_HALO, LANES), F32), pltpu.VMEM((nblk, tm, LANES), F32)],
        compiler_params=_params("arbitrary"),
        name="conv_module",
    )(x, _row(gpre), pw1_w.astype(BF16), _row(pw1_b), dw, _row(dw_b), _row(ln_g), _row(ln_b),
      pw2_w.astype(BF16), _row(pw2_b), _row(gpost))


def _chunk_pieces(tbl_ref, tile, n_experts, max_piece, visit):
    for e in range(n_experts):
        base, length, off = tbl_ref[tile, e], tbl_ref[tile, n_experts + e], tbl_ref[tile, 2 * n_experts + e]
        p = max_piece
        while p >= ROW_ALIGN:
            done = jnp.bitwise_and(length, ~(2 * p - 1))

            @pl.when(jnp.bitwise_and(length, p) != 0)
            def _(p=p, done=done):
                visit(p, pl.multiple_of(off + done, ROW_ALIGN), pl.multiple_of(base + done, ROW_ALIGN))

            p //= 2


def _dispatch_kernel(tbl_ref, fill_start_ref, fill_len_ref, x_ref, gpre_ref, metat_ref, xs_ref, xp_ref, zero_ref,
                     sem, zsem, *, n_experts):
    tm, d = x_ref.shape
    rows = xp_ref.shape[1]
    i = pl.program_id(0)
    last = pl.num_programs(0) - 1
    slot = i % 2

    hn = _rms(x_ref[...], gpre_ref[...]).astype(BF16)
    row_id = lax.broadcasted_iota(jnp.int32, (rows, tm), 0)
    place = metat_ref[TOP_K:2 * TOP_K, :].astype(jnp.int32)
    hit = (row_id == place[0:1, :]) | (row_id == place[1:2, :])
    perm = jnp.where(hit, 1.0, 0.0).astype(BF16)
    xp_ref[slot] = jnp.dot(perm, hn, preferred_element_type=F32)

    def chunk_copy(buf, p, tile_row, global_row):
        return pltpu.make_async_copy(xp_ref.at[buf, pl.ds(tile_row, p), :], xs_ref.at[pl.ds(global_row, p), :],
                                     sem.at[buf])

    _chunk_pieces(tbl_ref, i, n_experts, tm, lambda *a: chunk_copy(slot, *a).start())

    @pl.when(i > 0)
    def _():
        _chunk_pieces(tbl_ref, i - 1, n_experts, tm, lambda *a: chunk_copy(1 - slot, *a).wait())

    @pl.when(i == last)
    def _():
        zero_ref[...] = jnp.zeros_like(zero_ref)
        zrows = zero_ref.shape[0]
        for e in range(n_experts + 1):
            step = ROW_ALIGN if e < n_experts else zrows

            def zero_copy(r, e=e, step=step):
                row0 = pl.multiple_of(fill_start_ref[e] + r * step, step)
                return pltpu.make_async_copy(zero_ref.at[pl.ds(0, step), :], xs_ref.at[pl.ds(row0, step), :], zsem)

            lax.fori_loop(0, fill_len_ref[e], lambda r, c: (zero_copy(r).start(), c)[1], 0)
            lax.fori_loop(0, fill_len_ref[e], lambda r, c: (zero_copy(r).wait(), c)[1], 0)
        _chunk_pieces(tbl_ref, i, n_experts, tm, lambda *a: chunk_copy(slot, *a).wait())


def _dispatch(x, gpre, tbl, metat, fill_start, fill_len, *, rows, tm, tile_rows, zrows, n_experts):
    n, d = x.shape
    smem = pl.BlockSpec(memory_space=pltpu.SMEM)
    return pl.pallas_call(
        functools.partial(_dispatch_kernel, n_experts=n_experts),
        grid=(n // tm,),
        in_specs=[smem, smem, smem, pl.BlockSpec((tm, d), lambda i: (i, 0)), _full((1, d)),
                  pl.BlockSpec((SUBLANES, tm), lambda i: (i, 0))],
        out_specs=pl.BlockSpec(memory_space=pl.ANY),
        out_shape=jax.ShapeDtypeStruct((rows, d), F32),
        scratch_shapes=[pltpu.VMEM((2, tile_rows, d), F32), pltpu.VMEM((zrows, d), F32),
                        pltpu.SemaphoreType.DMA((2,)), pltpu.SemaphoreType.DMA],
        compiler_params=_params("arbitrary"),
        name="moe_dispatch",
    )(tbl, fill_start, fill_len, x, _row(gpre), metat)


def _combine_kernel(tbl_ref, x_ref, meta_ref, ys_ref, gpost_ref, o_ref, yp_ref, sem, *, n_experts):
    tm, d = x_ref.shape
    rows = yp_ref.shape[1]
    i = pl.program_id(0)
    slot = i % 2

    def chunk_copy(buf, p, tile_row, global_row):
        return pltpu.make_async_copy(ys_ref.at[pl.ds(global_row, p), :], yp_ref.at[buf, pl.ds(tile_row, p), :],
                                     sem.at[buf])

    @pl.when(i == 0)
    def _():
        yp_ref[...] = jnp.zeros_like(yp_ref)
        _chunk_pieces(tbl_ref, i, n_experts, tm, lambda *a: chunk_copy(slot, *a).start())

    @pl.when(i + 1 < pl.num_programs(0))
    def _():
        _chunk_pieces(tbl_ref, i + 1, n_experts, tm, lambda *a: chunk_copy(1 - slot, *a).start())

    _chunk_pieces(tbl_ref, i, n_experts, tm, lambda *a: chunk_copy(slot, *a).wait())

    yp = yp_ref[slot].astype(BF16)
    row_id = lax.broadcasted_iota(jnp.int32, (tm, rows), 1)
    meta = meta_ref[...]
    place = meta[:, TOP_K:2 * TOP_K].astype(jnp.int32)
    y = None
    for k in range(TOP_K):
        pick = jnp.where(row_id == place[:, k:k + 1], 1.0, 0.0).astype(BF16)
        yk = jnp.dot(pick, yp, preferred_element_type=F32) * meta[:, 2 * TOP_K + k:2 * TOP_K + k + 1]
        y = yk if y is None else y + yk
    o_ref[...] = x_ref[...] + _rms(y, gpost_ref[...])


def _combine(x, tbl, meta, ys, gpost, *, tm, tile_rows, n_experts):
    n, d = x.shape
    tok = pl.BlockSpec((tm, d), lambda i: (i, 0))
    return pl.pallas_call(
        functools.partial(_combine_kernel, n_experts=n_experts),
        grid=(n // tm,),
        in_specs=[
            pl.BlockSpec(memory_space=pltpu.SMEM),
            tok,
            pl.BlockSpec((tm, SUBLANES), lambda i: (i, 0)),
            pl.BlockSpec(memory_space=pl.ANY),
            _full((1, d)),
        ],
        out_specs=tok,
        out_shape=jax.ShapeDtypeStruct((n, d), F32),
        scratch_shapes=[pltpu.VMEM((2, tile_rows, d), F32), pltpu.SemaphoreType.DMA((2,))],
        compiler_params=_params("arbitrary"),
        name="moe_combine",
    )(tbl, x, meta, ys, _row(gpost))


def _moe_layer(x, meta, metat, cnt, gpre, w_gu, w_down, gpost, *, tm, te, fb):
    n, d = x.shape
    e = w_gu.shape[0]
    nt = n // tm

    counts = cnt.reshape(nt, SUBLANES, LANES)[:, :e, 0].astype(jnp.int32)
    chunk = (counts + ROW_ALIGN - 1) // ROW_ALIGN * ROW_ALIGN
    tile_off = jnp.cumsum(chunk, axis=1) - chunk
    total = jnp.sum(chunk, axis=0)
    padded = (total + te - 1) // te * te
    ends = jnp.cumsum(padded)
    starts = ends - padded
    base = starts[None, :] + jnp.cumsum(chunk, axis=0) - chunk
    tbl = jnp.concatenate([base, chunk, tile_off], axis=1)
    tile_rows = -(-(TOP_K * tm + e * ROW_ALIGN) // LANES) * LANES

    rows = TOP_K * n + e * ROW_ALIGN * nt + e * te
    rows = -(-rows // te) * te
    ntiles = (ends[-1] // te).reshape(1)
    tile_ids = jnp.arange(rows // te, dtype=jnp.int32)
    tile_expert = jnp.minimum(jnp.sum(tile_ids[:, None] >= (ends // te)[None, :], axis=1), e - 1).astype(jnp.int32)

    zrows = min(te, ZERO_ROWS)
    assert te % zrows == 0 and zrows % ROW_ALIGN == 0
    fill_start = jnp.concatenate([starts + total, ends[-1:]])
    fill_len = jnp.concatenate([(padded - total) // ROW_ALIGN, (rows - ends[-1:]) // zrows])
    xs = _dispatch(x, gpre, tbl, metat, fill_start, fill_len, rows=rows, tm=tm, tile_rows=tile_rows, zrows=zrows,
                   n_experts=e)
    ys = _experts_layer(xs, w_gu, w_down, tile_expert, ntiles, te=te, fb=fb)
    return _combine(x, tbl, meta, ys, gpost, tm=tm, tile_rows=tile_rows, n_experts=e)


def _tile(n, want):
    t = min(n, want)
    while n % t:
        t //= 2
    return t


@jax.jit
def kernel(x, mem, mem_norm_g, mix_pre_g, mix_post_g, xa_pre_g, xa_post_g, ffn_pre_g, ffn_post_g, xa_wq, xa_wkv, xa_wo, pool_w, pool_b, pool_scale, conv_pw1_w, conv_pw1_b, conv_dw_w, conv_dw_b, conv_ln_g, conv_ln_b, conv_pw2_w, conv_pw2_b, ffn_w_gu, ffn_w_down, moe_router, moe_w_gu, moe_w_down):
    b, s, d = x.shape
    depth = mix_pre_g.shape[0]
    f = ffn_w_down.shape[1]
    tm = _tile(s, 512)
    tf = _tile(s, 1024)
    fb = _tile(f, 512)
    kt, v = _memory_kv(mem, mem_norm_g, xa_wkv)
    h = x.reshape(b * s, d)
    for i in range(depth):
        j = i // 2
        xattn = functools.partial(_xattn_layer, kt=kt[i], v=v[i], wq=xa_wq[i], wo=xa_wo[i], gpre=xa_pre_g[i],
                                  gpost=xa_post_g[i], seq=s)
        if i % 2 == 0:
            h = xattn(h, pool=(mix_pre_g[i], pool_w[j], pool_b[j], pool_scale[j], mix_post_g[i]), tm=tf)
            h = _ffn_layer(h, ffn_pre_g[i], ffn_w_gu[j], ffn_w_down[j], ffn_post_g[i], tm=tf, fb=fb)
        else:
            h = _conv_layer(h, mix_pre_g[i], conv_pw1_w[j], conv_pw1_b[j], conv_dw_w[j], conv_dw_b[j], conv_ln_g[j],
                            conv_ln_b[j], conv_pw2_w[j], conv_pw2_b[j], mix_post_g[i], seq=s, tm=tm)
            h, meta, metat, cnt = xattn(h, route=(ffn_pre_g[i], moe_router[j]), tm=tm)
            h = _moe_layer(h, meta, metat, cnt, ffn_pre_g[i], moe_w_gu[j], moe_w_down[j], ffn_post_g[i], tm=tm, te=tf, fb=fb)
    return h.reshape(b, s, d)
```
